```python
import math
import jax, jax.numpy as jnp
from jax import lax
import numpy as np

D_MODEL = 1024
BATCH = 2
SEQ = 8192
DEPTH = 2
DEC_BATCH = 16
DEC_SEQ = 4096
PAST_LEN = 128

FNET_GROUPS = 4
FNET_GROUP_DIM = 64
FNET_WIDTH = FNET_GROUPS * FNET_GROUP_DIM
RWKV_HEADS = 8
RWKV_HEAD_DIM = 64
RWKV_WIDTH = RWKV_HEADS * RWKV_HEAD_DIM
N_DIR = 2
W_LORA = 64
A_LORA = 64
G_LORA = 128
RWKV_IN = 3 * RWKV_WIDTH + N_DIR * W_LORA + N_DIR * A_LORA + G_LORA
CONV_WIDTH = 256
CONV_KERNEL = 31
N_BRANCH = 3
IN_COLS = FNET_WIDTH + RWKV_IN + 2 * CONV_WIDTH + N_BRANCH * D_MODEL
D_FF = 2816
FFN_KERNEL = 3

RMS_EPS = 1e-6
LN_EPS = 1e-5
GN_EPS = 64e-5
DECAY_SCALE = math.exp(-0.5)

kernel_name = 'hybrid_fnet_rwkv7_conformer_encoder'


def rms_norm(x, g):
    xf = x.astype(jnp.float32)
    y = xf * lax.rsqrt(jnp.mean(xf * xf, axis=-1, keepdims=True) + RMS_EPS)
    return (y * g).astype(x.dtype)


def layer_norm(x, w, b):
    xf = x.astype(jnp.float32)
    mu = jnp.mean(xf, axis=-1, keepdims=True)
    var = jnp.mean(jnp.square(xf - mu), axis=-1, keepdims=True)
    return ((xf - mu) * lax.rsqrt(var + LN_EPS) * w + b).astype(x.dtype)


def depthwise_conv(u, w, b):
    k = w.shape[0]
    y = lax.conv_general_dilated(u, w[:, None, :], window_strides=(1,), padding=[(k // 2, k // 2)],
                                 dimension_numbers=('NWC', 'WIO', 'NWC'), feature_group_count=u.shape[-1])
    return y + b


def centred_token_shift(u, mu_prev, mu_next):
    u_prev = jnp.pad(u[:, :-1], ((0, 0), (1, 0), (0, 0)))
    u_next = jnp.pad(u[:, 1:], ((0, 0), (0, 1), (0, 0)))
    return u + mu_prev * (u_prev - u) + mu_next * (u_next - u)


def fourier_mix(u):
    bsz, seq = u.shape[:2]
    ug = u.astype(jnp.float32).reshape(bsz, seq, FNET_GROUPS, FNET_GROUP_DIM)
    y = jnp.fft.fftn(ug, axes=(1, 3), norm='ortho').real
    return y.reshape(bsz, seq, FNET_WIDTH).astype(u.dtype)


def directional_scan(r, decay, kk, kk_a, k_rep, v, reverse):
    bsz = r.shape[0]

    def step(state, inp):
        r_t, w_t, kk_t, kka_t, k_t, v_t = inp
        sa = jnp.einsum('bhvk,bhk->bhv', state, kk_t)
        state = (state * w_t[:, :, None, :] - sa[..., None] * kka_t[:, :, None, :]
                 + v_t[..., None] * k_t[:, :, None, :])
        return state, jnp.einsum('bhvk,bhk->bhv', state, r_t)

    xs = tuple(jnp.swapaxes(t, 0, 1) for t in (r, decay, kk, kk_a, k_rep, v))
    state0 = jnp.zeros((bsz, RWKV_HEADS, RWKV_HEAD_DIM, RWKV_HEAD_DIM), jnp.float32)
    _, o = lax.scan(step, state0, xs, reverse=reverse)
    return jnp.swapaxes(o, 0, 1)


def rwkv_time_mix(u, mu_prev, mu_next, w0, w_up, a0, a_up, g_up, k_k, k_a, r_k, gn_w, gn_b):
    bsz, seq = u.shape[:2]
    out_dtype = u.dtype
    u = centred_token_shift(u, mu_prev, mu_next).astype(jnp.float32)
    c0 = RWKV_WIDTH
    c1 = 3 * RWKV_WIDTH + N_DIR * W_LORA
    c2 = c1 + N_DIR * A_LORA
    r, k, v, wd, ad, gd = jnp.split(u, [c0, 2 * c0, 3 * c0, c1, c2], axis=-1)

    wd = jnp.tanh(wd.reshape(bsz, seq, N_DIR, W_LORA))
    decay = jnp.exp(-DECAY_SCALE * jax.nn.sigmoid(w0 + jnp.einsum('bsdr,drc->bsdc', wd, w_up)))
    ad = ad.reshape(bsz, seq, N_DIR, A_LORA)
    a = jax.nn.sigmoid(a0 + jnp.einsum('bsdr,drc->bsdc', ad, a_up))
    g = jax.nn.sigmoid(gd) @ g_up

    def heads(t):
        return t.reshape(t.shape[:2] + (RWKV_HEADS, RWKV_HEAD_DIM))

    rh, kh, vh = heads(r), heads(k), heads(v)
    kk = kh * k_k.reshape(RWKV_HEADS, RWKV_HEAD_DIM)
    kk = kk * lax.rsqrt(jnp.maximum(jnp.sum(kk * kk, axis=-1, keepdims=True), 1e-24))
    k_a_h = k_a.reshape(RWKV_HEADS, RWKV_HEAD_DIM)

    o = jnp.zeros_like(rh)
    for d in range(N_DIR):
        a_h = heads(a[:, :, d])
        k_rep = kh * (1.0 + (a_h - 1.0) * k_a_h)
        o = o + directional_scan(rh, heads(decay[:, :, d]), kk, kk * a_h, k_rep, vh, reverse=(d == 1))

    mu = jnp.mean(o, axis=-1, keepdims=True)
    var = jnp.mean(jnp.square(o - mu), axis=-1, keepdims=True)
    on = ((o - mu) * lax.rsqrt(var + GN_EPS)).reshape(bsz, seq, RWKV_WIDTH) * gn_w + gn_b
    bonus = (jnp.sum(rh * kh * r_k, axis=-1, keepdims=True) * vh).reshape(bsz, seq, RWKV_WIDTH)
    return ((on + bonus) * g).astype(out_dtype)


def conformer_conv(u, dw_w, dw_b, ln_w, ln_b):
    h = u[..., :CONV_WIDTH] * jax.nn.sigmoid(u[..., CONV_WIDTH:])
    h = depthwise_conv(h, dw_w, dw_b)
    h = layer_norm(h, ln_w, ln_b)
    return h * jax.nn.sigmoid(h)


def mixer_block(xn, p, l):
    u = xn @ p['w_in'][l]
    s0 = FNET_WIDTH
    s1 = s0 + RWKV_IN
    s2 = s1 + 2 * CONV_WIDTH
    u_a, u_b, u_c, u_g = jnp.split(u, [s0, s1, s2], axis=-1)
    y_a = fourier_mix(u_a) @ p['fnet_w'][l]
    y_b = rwkv_time_mix(u_b, p['rwkv_mu_prev'][l], p['rwkv_mu_next'][l], p['rwkv_w0'][l], p['rwkv_w_up'][l],
                        p['rwkv_a0'][l], p['rwkv_a_up'][l], p['rwkv_g_up'][l], p['rwkv_k_k'][l], p['rwkv_k_a'][l],
                        p['rwkv_r_k'][l], p['rwkv_gn_w'][l], p['rwkv_gn_b'][l]) @ p['rwkv_w_o'][l]
    y_c = conformer_conv(u_c, p['conv_dw_w'][l], p['conv_dw_b'][l], p['conv_ln_w'][l],
                         p['conv_ln_b'][l]) @ p['conv_w_o'][l]
    g_a, g_b, g_c = jnp.split(jax.nn.sigmoid(u_g), N_BRANCH, axis=-1)
    return (g_a * y_a + g_b * y_b + g_c * y_c) @ p['mix_w_out'][l]


def conv_ffn(xn, w_up, dw_w, dw_b, w_down):
    h, gate = jnp.split(xn @ w_up, 2, axis=-1)
    h = depthwise_conv(h, dw_w, dw_b)
    return (jax.nn.gelu(h, approximate=False) * gate) @ w_down


def encoder(x, p):
    for l in range(DEPTH):
        x = x + mixer_block(rms_norm(x, p['attn_norm_g'][l]), p, l)
        x = x + conv_ffn(rms_norm(x, p['ffn_norm_g'][l]), p['ffn_w_up'][l], p['ffn_dw_w'][l],
                         p['ffn_dw_b'][l], p['ffn_w_down'][l])
    return rms_norm(x, p['final_norm_g'])


def setup_inputs(seed: int = 0) -> dict:
    key = jax.random.key(seed)
    ks = iter(jax.random.split(key, 40))

    def nrm(shape, scale):
        return jax.random.normal(next(ks), shape, jnp.float32) * scale

    def uni(shape, lo, hi):
        return jax.random.uniform(next(ks), shape, jnp.float32, lo, hi)

    L = DEPTH
    return {
        'x_prompt': nrm((BATCH, SEQ, D_MODEL), 1.0),
        'x_sample': nrm((DEC_BATCH, DEC_SEQ, D_MODEL), 1.0),
        'attn_norm_g': 1.0 + nrm((L, D_MODEL), 0.02),
        'w_in': nrm((L, D_MODEL, IN_COLS), D_MODEL ** -0.5),
        'fnet_w': nrm((L, FNET_WIDTH, D_MODEL), FNET_WIDTH ** -0.5),
        'rwkv_mu_prev': uni((L, RWKV_IN), 0.0, 0.5),
        'rwkv_mu_next': uni((L, RWKV_IN), 0.0, 0.5),
        'rwkv_w0': nrm((L, N_DIR, RWKV_WIDTH), 0.5),
        'rwkv_w_up': nrm((L, N_DIR, W_LORA, RWKV_WIDTH), 0.1 * W_LORA ** -0.5),
        'rwkv_a0': nrm((L, N_DIR, RWKV_WIDTH), 0.5),
        'rwkv_a_up': nrm((L, N_DIR, A_LORA, RWKV_WIDTH), 0.5 * A_LORA ** -0.5),
        'rwkv_g_up': nrm((L, G_LORA, RWKV_WIDTH), G_LORA ** -0.5),
        'rwkv_k_k': 0.85 + nrm((L, RWKV_WIDTH), 0.05),
        'rwkv_k_a': 1.0 + nrm((L, RWKV_WIDTH), 0.05),
        'rwkv_r_k': nrm((L, RWKV_HEADS, RWKV_HEAD_DIM), 0.1),
        'rwkv_gn_w': 1.0 + nrm((L, RWKV_WIDTH), 0.02),
        'rwkv_gn_b': nrm((L, RWKV_WIDTH), 0.01),
        'rwkv_w_o': nrm((L, RWKV_WIDTH, D_MODEL), RWKV_WIDTH ** -0.5),
        'conv_dw_w': nrm((L, CONV_KERNEL, CONV_WIDTH), CONV_KERNEL ** -0.5),
        'conv_dw_b': nrm((L, CONV_WIDTH), 0.01),
        'conv_ln_w': 1.0 + nrm((L, CONV_WIDTH), 0.02),
        'conv_ln_b': nrm((L, CONV_WIDTH), 0.01),
        'conv_w_o': nrm((L, CONV_WIDTH, D_MODEL), CONV_WIDTH ** -0.5),
        'mix_w_out': nrm((L, D_MODEL, D_MODEL), 0.5 * D_MODEL ** -0.5),
        'ffn_norm_g': 1.0 + nrm((L, D_MODEL), 0.02),
        'ffn_w_up': nrm((L, D_MODEL, 2 * D_FF), D_MODEL ** -0.5),
        'ffn_dw_w': nrm((L, FFN_KERNEL, D_FF), FFN_KERNEL ** -0.5),
        'ffn_dw_b': nrm((L, D_FF), 0.01),
        'ffn_w_down': nrm((L, D_FF, D_MODEL), 0.5 * D_FF ** -0.5),
        'final_norm_g': 1.0 + nrm((D_MODEL,), 0.02),
    }


def reference(x_prompt, x_sample, attn_norm_g, w_in, fnet_w, rwkv_mu_prev, rwkv_mu_next, rwkv_w0, rwkv_w_up,
              rwkv_a0, rwkv_a_up, rwkv_g_up, rwkv_k_k, rwkv_k_a, rwkv_r_k, rwkv_gn_w, rwkv_gn_b, rwkv_w_o,
              conv_dw_w, conv_dw_b, conv_ln_w, conv_ln_b, conv_w_o, mix_w_out, ffn_norm_g, ffn_w_up,
              ffn_dw_w, ffn_dw_b, ffn_w_down, final_norm_g):
    p = dict(attn_norm_g=attn_norm_g, w_in=w_in, fnet_w=fnet_w, rwkv_mu_prev=rwkv_mu_prev,
             rwkv_mu_next=rwkv_mu_next, rwkv_w0=rwkv_w0, rwkv_w_up=rwkv_w_up, rwkv_a0=rwkv_a0,
             rwkv_a_up=rwkv_a_up, rwkv_g_up=rwkv_g_up, rwkv_k_k=rwkv_k_k, rwkv_k_a=rwkv_k_a, rwkv_r_k=rwkv_r_k,
             rwkv_gn_w=rwkv_gn_w, rwkv_gn_b=rwkv_gn_b, rwkv_w_o=rwkv_w_o, conv_dw_w=conv_dw_w,
             conv_dw_b=conv_dw_b, conv_ln_w=conv_ln_w, conv_ln_b=conv_ln_b, conv_w_o=conv_w_o,
             mix_w_out=mix_w_out, ffn_norm_g=ffn_norm_g, ffn_w_up=ffn_w_up, ffn_dw_w=ffn_dw_w,
             ffn_dw_b=ffn_dw_b, ffn_w_down=ffn_w_down, final_norm_g=final_norm_g)
    y_prompt = encoder(x_prompt, p)
    y_sample = encoder(x_sample, p)
    return (y_prompt, y_sample)
```

```python
import functools
import math

import numpy as np
import jax
import jax.numpy as jnp
from jax import lax
from jax.experimental import pallas as pl
from jax.experimental.pallas import tpu as pltpu

D_MODEL = 1024
FNET_GROUPS = 4
FNET_GROUP_DIM = 64
FNET_WIDTH = 256
RWKV_HEADS = 8
RWKV_HEAD_DIM = 64
RWKV_WIDTH = 512
N_DIR = 2
W_LORA = 64
A_LORA = 64
G_LORA = 128
RWKV_IN = 1920
CONV_WIDTH = 256
CONV_KERNEL = 31
IN_COLS = 5760
D_FF = 2816
RMS_EPS = 1e-6
LN_EPS = 1e-5
GN_EPS = 64e-5
DECAY_SCALE = math.exp(-0.5)

COL_A = 0
COL_B = FNET_WIDTH
COL_C = COL_B + RWKV_IN
COL_G = COL_C + 2 * CONV_WIDTH

BF = jnp.bfloat16
F32 = jnp.float32

TOKEN_TILE = 512
SCAN_TILE = 512
CHUNK = 64
FFT_N2 = 64
FFT_COLS = 2048
FFT_K1_BLOCK = 8
VMEM_LIMIT = 56 * 1024 * 1024
FF_BLOCKS = ((0, 1024), (1024, 2048), (2048, 2816))


def _cparams(sem):
    return pltpu.CompilerParams(dimension_semantics=sem, vmem_limit_bytes=VMEM_LIMIT)


def _bdot(a, b):
    return jnp.dot(a.astype(BF), b.astype(BF), preferred_element_type=F32)


def _split_dot(a, b_exact):
    ah = a.astype(BF)
    al = (a - ah.astype(F32)).astype(BF)
    return (jnp.dot(ah, b_exact, preferred_element_type=F32)
            + jnp.dot(al, b_exact, preferred_element_type=F32))


def _dot_t0(a, b):
    return lax.dot_general(a, b, (((0,), (0,)), ((), ())), preferred_element_type=F32)


def _dot_t1(a, b):
    return lax.dot_general(a, b, (((1,), (1,)), ((), ())), preferred_element_type=F32)


def _sigmoid(x):
    return 1.0 / (1.0 + jnp.exp(-x))


def _rms(x, g):
    return x * lax.rsqrt(jnp.mean(x * x, axis=-1, keepdims=True) + RMS_EPS) * g


def _const_spec(shape):
    nd = len(shape)
    return pl.BlockSpec(shape, lambda *_: (0,) * nd)


def _mixer_in_kernel(x_ref, g_ref, w_ref, ua_ref, ub_ref, hc_ref, gate_ref):
    xn = _rms(x_ref[...], g_ref[...]).astype(BF)
    ua_ref[...] = jnp.dot(xn, w_ref[:, COL_A:COL_B], preferred_element_type=F32)
    for lo, hi in ((0, 1024), (1024, RWKV_IN)):
        ub_ref[:, lo:hi] = jnp.dot(xn, w_ref[:, COL_B + lo:COL_B + hi], preferred_element_type=F32)
    uc = jnp.dot(xn, w_ref[:, COL_C:COL_G], preferred_element_type=F32)
    hc_ref[...] = uc[:, :CONV_WIDTH] * _sigmoid(uc[:, CONV_WIDTH:])
    for j in range(3):
        lo = j * D_MODEL
        ug = jnp.dot(xn, w_ref[:, COL_G + lo:COL_G + lo + D_MODEL], preferred_element_type=F32)
        gate_ref[:, lo:lo + D_MODEL] = _sigmoid(ug).astype(BF)


def mixer_in(x, g, w_in_bf):
    t = x.shape[0]
    tm = TOKEN_TILE
    row = lambda w: pl.BlockSpec((tm, w), lambda i: (i, 0))
    return pl.pallas_call(
        _mixer_in_kernel,
        grid=(t // tm,),
        in_specs=[row(D_MODEL), _const_spec((1, D_MODEL)), _const_spec((D_MODEL, IN_COLS))],
        out_specs=[row(FNET_WIDTH), row(RWKV_IN), row(CONV_WIDTH), row(3 * D_MODEL)],
        out_shape=[jax.ShapeDtypeStruct((t, FNET_WIDTH), F32), jax.ShapeDtypeStruct((t, RWKV_IN), F32),
                   jax.ShapeDtypeStruct((t, CONV_WIDTH), F32), jax.ShapeDtypeStruct((t, 3 * D_MODEL), BF)],
        compiler_params=_cparams(("parallel",)),
        name="mixer_in",
    )(x, g.reshape(1, D_MODEL), w_in_bf)


def _fft_tables(seq):
    n2 = FFT_N2
    n1 = seq // n2
    k1 = np.arange(n1)[:, None].astype(np.float64)
    m1 = np.arange(n1)[None, :].astype(np.float64)
    ang1 = 2.0 * np.pi * ((k1 * m1) % n1) / n1
    f1 = np.concatenate([np.cos(ang1), -np.sin(ang1)], axis=0)
    m2 = np.arange(n2)[None, :].astype(np.float64)
    angt = 2.0 * np.pi * ((k1 * m2) % seq) / seq
    tr, ti = np.cos(angt), -np.sin(angt)
    k2 = np.arange(n2)[:, None].astype(np.float64)
    ang2 = 2.0 * np.pi * ((k2 * m2) % n2) / n2
    c2, s2 = np.cos(ang2), np.sin(ang2)
    f2 = np.block([[c2, s2], [-s2, c2]])
    q = np.arange(FNET_GROUP_DIM)
    angc = 2.0 * np.pi * ((q[:, None] * q[None, :]) % FNET_GROUP_DIM) / FNET_GROUP_DIM
    scale = 1.0 / math.sqrt(seq * FNET_GROUP_DIM)
    eye = np.eye(FNET_GROUPS)
    cd = np.concatenate([np.kron(eye, np.cos(angc)), np.kron(eye, np.sin(angc))], axis=0) * scale
    tr = jnp.repeat(jnp.asarray(tr, F32), FNET_WIDTH, axis=1)
    ti = jnp.repeat(jnp.asarray(ti, F32), FNET_WIDTH, axis=1)
    return (jnp.asarray(f1, BF), tr, ti, jnp.asarray(f2, BF), jnp.asarray(cd, BF))


def _fft1_kernel(x_ref, f1_ref, tr_ref, ti_ref, o_ref):
    n1 = x_ref.shape[1]
    res = jnp.dot(f1_ref[...], x_ref[0].astype(BF), preferred_element_type=F32)
    ar, ai = res[:n1], res[n1:]
    tr, ti = tr_ref[...], ti_ref[...]
    o_ref[0, 0] = ar * tr - ai * ti
    o_ref[0, 1] = ar * ti + ai * tr


def _fft2_kernel(z_ref, f2_ref, cd_ref, o_ref):
    n2 = FFT_N2
    for i in range(FFT_K1_BLOCK):
        z = jnp.concatenate([z_ref[0, 0, i], z_ref[0, 1, i]], axis=0).astype(BF)
        g = jnp.dot(f2_ref[...], z, preferred_element_type=F32)
        y = (jnp.dot(g[:n2].astype(BF), cd_ref[:FNET_WIDTH], preferred_element_type=F32)
             + jnp.dot(g[n2:].astype(BF), cd_ref[FNET_WIDTH:], preferred_element_type=F32))
        o_ref[0, :, i * FNET_WIDTH:(i + 1) * FNET_WIDTH] = y


def fourier_mix(ua, bsz, seq, tables):
    f1, tr, ti, f2, cd = tables
    n2 = FFT_N2
    n1 = seq // n2
    cols = n2 * FNET_WIDTH
    cb = FFT_COLS
    x = ua.reshape(bsz, n1, cols)
    z = pl.pallas_call(
        _fft1_kernel,
        grid=(bsz, cols // cb),
        in_specs=[pl.BlockSpec((1, n1, cb), lambda b, j: (b, 0, j)),
                  _const_spec((2 * n1, n1)),
                  pl.BlockSpec((n1, cb), lambda b, j: (0, j)),
                  pl.BlockSpec((n1, cb), lambda b, j: (0, j))],
        out_specs=pl.BlockSpec((1, 2, n1, cb), lambda b, j: (b, 0, 0, j)),
        out_shape=jax.ShapeDtypeStruct((bsz, 2, n1, cols), F32),
        compiler_params=_cparams(("parallel", "parallel")),
        name="fft1",
    )(x, f1, tr, ti)
    z = z.reshape(bsz, 2, n1, n2, FNET_WIDTH)
    kb = FFT_K1_BLOCK
    y = pl.pallas_call(
        _fft2_kernel,
        grid=(bsz, n1 // kb),
        in_specs=[pl.BlockSpec((1, 2, kb, n2, FNET_WIDTH), lambda b, j: (b, 0, j, 0, 0)),
                  _const_spec((2 * n2, 2 * n2)),
                  _const_spec((2 * FNET_WIDTH, FNET_WIDTH))],
        out_specs=pl.BlockSpec((1, n2, kb * FNET_WIDTH), lambda b, j: (b, 0, j)),
        out_shape=jax.ShapeDtypeStruct((bsz, n2, n1 * FNET_WIDTH), F32),
        compiler_params=_cparams(("parallel", "parallel")),
        name="fft2",
    )(z, f2, cd)
    return y.reshape(bsz * seq, FNET_WIDTH)


def _head_ones():
    h = np.arange(RWKV_WIDTH) // RWKV_HEAD_DIM
    return jnp.asarray((h[:, None] == h[None, :]).astype(np.float32), BF)


def _rwkv_prep_kernel(tiles_per_seq, u_ref, up_ref, un_ref, mup_ref, mun_ref, w0_ref, wup_ref, a0_ref, aup_ref,
                      gup_ref, kk_w_ref, ka_ref, rk_ref, ones_ref,
                      r_ref, kkn_ref, v_ref, krep_ref, kka_ref, logw_ref, bonus_ref, g_ref, buf):
    tm = u_ref.shape[0]
    i = pl.program_id(0)
    first = (i % tiles_per_seq) == 0
    last = (i % tiles_per_seq) == tiles_per_seq - 1
    buf[0:8] = jnp.where(first, 0.0, up_ref[...])
    buf[8:tm + 8] = u_ref[...]
    buf[tm + 8:tm + 16] = jnp.where(last, 0.0, un_ref[...])
    mup, mun = mup_ref[...], mun_ref[...]

    def shifted(lo, hi):
        u = buf[8:tm + 8, lo:hi]
        return (u + mup[:, lo:hi] * (buf[7:tm + 7, lo:hi] - u) + mun[:, lo:hi] * (buf[9:tm + 9, lo:hi] - u))

    c0 = RWKV_WIDTH
    r = shifted(0, c0)
    k = shifted(c0, 2 * c0)
    v = shifted(2 * c0, 3 * c0)
    wd = shifted(3 * c0, 3 * c0 + 128)
    ad = shifted(3 * c0 + 128, 3 * c0 + 256)
    gd = shifted(3 * c0 + 256, 3 * c0 + 384)

    ones = ones_ref[...]
    r_ref[...] = r
    v_ref[...] = v
    kk = k * kk_w_ref[...]
    ss = _split_dot(kk * kk, ones)
    kk = kk * lax.rsqrt(jnp.maximum(ss, 1e-24))
    kkn_ref[...] = kk
    bonus_ref[...] = _split_dot(r * k * rk_ref[...], ones) * v
    g_ref[...] = _bdot(_sigmoid(gd), gup_ref[...])

    wlogit = _bdot(jnp.tanh(wd), wup_ref[...])
    alogit = _bdot(ad, aup_ref[...])
    ka = ka_ref[...]
    for d in range(N_DIR):
        sl = slice(d * c0, (d + 1) * c0)
        logw_ref[d] = -DECAY_SCALE * _sigmoid(w0_ref[:, sl] + wlogit[:, sl])
        a = _sigmoid(a0_ref[:, sl] + alogit[:, sl])
        krep_ref[d] = k * (1.0 + (a - 1.0) * ka)
        kka_ref[d] = kk * a


def _lora_block(w):
    z = jnp.zeros_like(w[0])
    return jnp.concatenate([jnp.concatenate([w[0], z], axis=1), jnp.concatenate([z, w[1]], axis=1)], axis=0)


def rwkv_prep(ub, seq, p, l):
    t = ub.shape[0]
    tm = TOKEN_TILE
    nb8 = tm // 8
    row = lambda w: pl.BlockSpec((tm, w), lambda i: (i, 0))
    row2 = pl.BlockSpec((N_DIR, tm, RWKV_WIDTH), lambda i: (0, i, 0))
    vec = lambda a: a.reshape(1, -1).astype(F32)
    sds = jax.ShapeDtypeStruct
    outs = pl.pallas_call(
        functools.partial(_rwkv_prep_kernel, seq // tm),
        grid=(t // tm,),
        in_specs=[row(RWKV_IN),
                  pl.BlockSpec((8, RWKV_IN), lambda i: (jnp.maximum(i * nb8 - 1, 0), 0)),
                  pl.BlockSpec((8, RWKV_IN), lambda i: (jnp.minimum((i + 1) * nb8, t // 8 - 1), 0)),
                  _const_spec((1, RWKV_IN)), _const_spec((1, RWKV_IN)),
                  _const_spec((1, 2 * RWKV_WIDTH)), _const_spec((2 * W_LORA, 2 * RWKV_WIDTH)),
                  _const_spec((1, 2 * RWKV_WIDTH)), _const_spec((2 * A_LORA, 2 * RWKV_WIDTH)),
                  _const_spec((G_LORA, RWKV_WIDTH)),
                  _const_spec((1, RWKV_WIDTH)), _const_spec((1, RWKV_WIDTH)), _const_spec((1, RWKV_WIDTH)),
                  _const_spec((RWKV_WIDTH, RWKV_WIDTH))],
        out_specs=[row(RWKV_WIDTH), row(RWKV_WIDTH), row(RWKV_WIDTH), row2, row2, row2,
                   row(RWKV_WIDTH), row(RWKV_WIDTH)],
        out_shape=[sds((t, RWKV_WIDTH), F32)] * 3 + [sds((N_DIR, t, RWKV_WIDTH), F32)] * 3
                  + [sds((t, RWKV_WIDTH), F32)] * 2,
        scratch_shapes=[pltpu.VMEM((tm + 16, RWKV_IN), F32)],
        compiler_params=_cparams(("parallel",)),
        name="rwkv_prep",
    )(ub, ub, ub, vec(p['rwkv_mu_prev'][l]), vec(p['rwkv_mu_next'][l]),
      vec(p['rwkv_w0'][l]), _lora_block(p['rwkv_w_up'][l]).astype(BF),
      vec(p['rwkv_a0'][l]), _lora_block(p['rwkv_a_up'][l]).astype(BF),
      p['rwkv_g_up'][l].astype(BF), vec(p['rwkv_k_k'][l]), vec(p['rwkv_k_a'][l]), vec(p['rwkv_r_k'][l]),
      _head_ones())
    return outs


N_PAIR = RWKV_HEADS // 2
PAIR = 2 * RWKV_HEAD_DIM


def _tri_inverse(n, row, col):
    size = n.shape[0]
    eye = jnp.where(row == col, 1.0, 0.0)
    t = None
    b = 1
    while b < size:
        off = jnp.where(((row // (2 * b)) == (col // (2 * b))) & ((row // b) != (col // b)), n, 0.0)
        if t is None:
            t = eye - off
        else:
            tb = t.astype(BF)
            t = t - jnp.dot(jnp.dot(tb, off.astype(BF), preferred_element_type=F32).astype(BF), tb,
                            preferred_element_type=F32)
        b *= 2
    return t


def _scan_kernel(r_ref, kk_ref, v_ref, krep_ref, kka_ref, logw_ref, o_ref,
                 h_ref, ar_s, t_s, mrb_s, x0_s, o0_s, bg_s, kv_s, dec_s):
    L = CHUNK
    hd = RWKV_HEAD_DIM
    nch = r_ref.shape[0] // L
    d = pl.program_id(1)

    @pl.when(pl.program_id(2) == 0)
    def _():
        h_ref[...] = jnp.zeros_like(h_ref)

    sgn = 1 - 2 * d
    row = lax.broadcasted_iota(jnp.int32, (L, L), 0)
    col = lax.broadcasted_iota(jnp.int32, (L, L), 1)
    diff = (row - col) * sgn
    strict = diff > 0
    incl = diff >= 0
    tri = jnp.where(incl, 1.0, 0.0).astype(BF)
    row2 = lax.broadcasted_iota(jnp.int32, (L, PAIR), 0)
    lane2 = lax.broadcasted_iota(jnp.int32, (L, PAIR), 1)
    diff2 = (row2 - (lane2 % L)) * sgn
    kpart = lane2 >= hd
    half0 = lane2 < hd
    prow = lax.broadcasted_iota(jnp.int32, (PAIR, PAIR), 0)
    pcol = lax.broadcasted_iota(jnp.int32, (PAIR, PAIR), 1)
    same_head = (prow // hd) == (pcol // hd)
    ones_lp = jnp.ones((L, PAIR), BF)

    def phase1_body(j, carry):
        sl = pl.ds(pl.multiple_of(j * L, L), L)
        lw = logw_ref[0, sl, :]
        lwh = lw.astype(BF)
        lwl = (lw - lwh.astype(F32)).astype(BF)
        g = (jnp.dot(tri, lwh, preferred_element_type=F32) + jnp.dot(tri, lwl, preferred_element_type=F32))
        e = g - lw
        gtot = jnp.sum(lw, axis=0, keepdims=True)
        e_e = jnp.exp(e)
        e_mg = jnp.exp(-g)
        e_gt = jnp.exp(gtot - g)
        kk = kk_ref[sl, :]
        kka = kka_ref[0, sl, :]
        krep = krep_ref[0, sl, :]
        at = kk * e_e
        bh = kka * e_mg
        kh = krep * e_mg
        rh = r_ref[sl, :] * jnp.exp(g)
        bg = kka * e_gt
        kg = krep * e_gt
        vv = v_ref[sl, :]
        for p in range(N_PAIR):
            ps = slice(p * PAIR, (p + 1) * PAIR)
            atp, rhp = at[:, ps], rh[:, ps]
            lhs = jnp.concatenate([jnp.where(half0, atp, 0.0), jnp.where(half0, 0.0, atp),
                                   jnp.where(half0, rhp, 0.0), jnp.where(half0, 0.0, rhp)], axis=0).astype(BF)
            rhs = jnp.concatenate([bh[:, ps], kh[:, ps]], axis=0).astype(BF)
            res = _dot_t1(lhs, rhs)
            vp = vv[:, ps].astype(BF)
            v2 = jnp.concatenate([vp, vp], axis=0)
            x0 = []
            o0 = []
            for hh in range(2):
                ma = res[hh * L:(hh + 1) * L]
                mr = res[(2 + hh) * L:(3 + hh) * L]
                nmat = jnp.where(strict, ma[:, :L], 0.0)
                t_s[j, 2 * p + hh] = _tri_inverse(nmat, row, col).astype(BF)
                mrb_s[j, 2 * p + hh] = jnp.where(incl, mr[:, :L], 0.0).astype(BF)
                lx = jnp.where((diff2 > 0) & kpart, ma, 0.0).astype(BF)
                lo = jnp.where((diff2 >= 0) & kpart, mr, 0.0).astype(BF)
                x0.append(jnp.dot(lx, v2, preferred_element_type=F32))
                o0.append(jnp.dot(lo, v2, preferred_element_type=F32))
            x0_s[j, p] = jnp.where(half0, x0[0], x0[1])
            o0_s[j, p] = jnp.where(half0, o0[0], o0[1])
            ar_s[j, p] = jnp.concatenate([atp, rhp], axis=0).astype(BF)
            bg_s[j, p] = bg[:, ps].astype(BF)
            kv_s[j, p] = jnp.where(same_head, _dot_t0(kg[:, ps].astype(BF), vp), 0.0)
            gcol = _dot_t0(lwh[:, ps], ones_lp) + _dot_t0(lwl[:, ps], ones_lp)
            dec_s[j, p] = jnp.exp(gcol)
        return carry

    lax.fori_loop(0, nch, phase1_body, 0)

    def phase2_body(jj, carry):
        j = jj + d * (nch - 1 - 2 * jj)
        sl = pl.ds(pl.multiple_of(j * L, L), L)
        for p in range(N_PAIR):
            hp = h_ref[p]
            arh = jnp.dot(ar_s[j, p], hp.astype(BF), preferred_element_type=F32)
            xb = (arh[:L] + x0_s[j, p]).astype(BF)
            u = jnp.where(half0, jnp.dot(t_s[j, 2 * p], xb, preferred_element_type=F32),
                          jnp.dot(t_s[j, 2 * p + 1], xb, preferred_element_type=F32))
            ub = u.astype(BF)
            o = arh[L:] + o0_s[j, p] - jnp.where(
                half0, jnp.dot(mrb_s[j, 2 * p], ub, preferred_element_type=F32),
                jnp.dot(mrb_s[j, 2 * p + 1], ub, preferred_element_type=F32))
            o_ref[0, sl, p * PAIR:(p + 1) * PAIR] = o
            bu = _dot_t0(bg_s[j, p], ub)
            h_ref[p] = dec_s[j, p] * hp + kv_s[j, p] - jnp.where(same_head, bu, 0.0)
        return carry

    lax.fori_loop(0, nch, phase2_body, 0)


def rwkv_scan(r, kk, v, krep, kka, logw, bsz, seq):
    t = r.shape[0]
    ts = SCAN_TILE
    nc = seq // ts
    nch = ts // CHUNK

    def tblk(b, d, c):
        return b * nc + c + d * (nc - 1 - 2 * c)

    shared = pl.BlockSpec((ts, RWKV_WIDTH), lambda b, d, c: (tblk(b, d, c), 0))
    perdir = pl.BlockSpec((1, ts, RWKV_WIDTH), lambda b, d, c: (d, tblk(b, d, c), 0))
    return pl.pallas_call(
        _scan_kernel,
        grid=(bsz, N_DIR, nc),
        in_specs=[shared, shared, shared, perdir, perdir, perdir],
        out_specs=perdir,
        out_shape=jax.ShapeDtypeStruct((N_DIR, t, RWKV_WIDTH), F32),
        scratch_shapes=[pltpu.VMEM((N_PAIR, PAIR, PAIR), F32),
                        pltpu.VMEM((nch, N_PAIR, 2 * CHUNK, PAIR), BF),
                        pltpu.VMEM((nch, RWKV_HEADS, CHUNK, CHUNK), BF),
                        pltpu.VMEM((nch, RWKV_HEADS, CHUNK, CHUNK), BF),
                        pltpu.VMEM((nch, N_PAIR, CHUNK, PAIR), F32),
                        pltpu.VMEM((nch, N_PAIR, CHUNK, PAIR), F32),
                        pltpu.VMEM((nch, N_PAIR, CHUNK, PAIR), BF),
                        pltpu.VMEM((nch, N_PAIR, PAIR, PAIR), F32),
                        pltpu.VMEM((nch, N_PAIR, PAIR, PAIR), F32)],
        compiler_params=_cparams(("parallel", "arbitrary", "arbitrary")),
        name="rwkv_scan",
    )(r, kk, v, krep, kka, logw)


CONV_HALO = 16


def _conv_kernel(tiles_per_seq, h_ref, hp_ref, hn_ref, w_ref, b_ref, lnw_ref, lnb_ref, o_ref, buf):
    tm = h_ref.shape[0]
    i = pl.program_id(0)
    first = (i % tiles_per_seq) == 0
    last = (i % tiles_per_seq) == tiles_per_seq - 1
    buf[0:CONV_HALO] = jnp.where(first, 0.0, hp_ref[...])
    buf[CONV_HALO:tm + CONV_HALO] = h_ref[...]
    buf[tm + CONV_HALO:tm + 2 * CONV_HALO] = jnp.where(last, 0.0, hn_ref[...])
    acc = jnp.zeros((tm, CONV_WIDTH), F32) + b_ref[...]
    for k in range(CONV_KERNEL):
        off = CONV_HALO - CONV_KERNEL // 2 + k
        acc = acc + w_ref[k:k + 1, :] * buf[off:off + tm, :]
    mu = jnp.mean(acc, axis=-1, keepdims=True)
    xc = acc - mu
    var = jnp.mean(xc * xc, axis=-1, keepdims=True)
    y = xc * lax.rsqrt(var + LN_EPS) * lnw_ref[...] + lnb_ref[...]
    o_ref[...] = y * _sigmoid(y)


def conformer_conv(hc, seq, p, l):
    t = hc.shape[0]
    tm = TOKEN_TILE
    nbh = tm // CONV_HALO
    row = pl.BlockSpec((tm, CONV_WIDTH), lambda i: (i, 0))
    vec = lambda a: a.reshape(1, -1).astype(F32)
    return pl.pallas_call(
        functools.partial(_conv_kernel, seq // tm),
        grid=(t // tm,),
        in_specs=[row,
                  pl.BlockSpec((CONV_HALO, CONV_WIDTH), lambda i: (jnp.maximum(i * nbh - 1, 0), 0)),
                  pl.BlockSpec((CONV_HALO, CONV_WIDTH),
                               lambda i: (jnp.minimum((i + 1) * nbh, t // CONV_HALO - 1), 0)),
                  _const_spec((CONV_KERNEL, CONV_WIDTH)), _const_spec((1, CONV_WIDTH)),
                  _const_spec((1, CONV_WIDTH)), _const_spec((1, CONV_WIDTH))],
        out_specs=row,
        out_shape=jax.ShapeDtypeStruct((t, CONV_WIDTH), F32),
        scratch_shapes=[pltpu.VMEM((tm + 2 * CONV_HALO, CONV_WIDTH), F32)],
        compiler_params=_cparams(("parallel",)),
        name="conformer_conv",
    )(hc, hc, hc, p['conv_dw_w'][l], vec(p['conv_dw_b'][l]), vec(p['conv_ln_w'][l]), vec(p['conv_ln_b'][l]))


def _merge_kernel(x_ref, ya_ref, o_ref, bonus_ref, g_ref, yc_ref, gate_ref, ones_ref, gnw_ref, gnb_ref,
                  wa_ref, wb_ref, wc_ref, wo_ref, out_ref):
    ones = ones_ref[...]
    o = o_ref[0] + o_ref[1]
    inv = 1.0 / RWKV_HEAD_DIM
    mu = _split_dot(o, ones) * inv
    oc = o - mu
    var = _split_dot(oc * oc, ones) * inv
    on = oc * lax.rsqrt(var + GN_EPS) * gnw_ref[...] + gnb_ref[...]
    yb = (on + bonus_ref[...]) * g_ref[...]
    m = gate_ref[:, 0:D_MODEL].astype(F32) * _bdot(ya_ref[...], wa_ref[...])
    m = m + gate_ref[:, D_MODEL:2 * D_MODEL].astype(F32) * _bdot(yb, wb_ref[...])
    m = m + gate_ref[:, 2 * D_MODEL:3 * D_MODEL].astype(F32) * _bdot(yc_ref[...], wc_ref[...])
    out_ref[...] = x_ref[...] + _bdot(m, wo_ref[...])


def merge(x, ya, o, bonus, g, yc, gates, p, l):
    t = x.shape[0]
    tm = TOKEN_TILE
    row = lambda w: pl.BlockSpec((tm, w), lambda i: (i, 0))
    vec = lambda a: a.reshape(1, -1).astype(F32)
    return pl.pallas_call(
        _merge_kernel,
        grid=(t // tm,),
        in_specs=[row(D_MODEL), row(FNET_WIDTH),
                  pl.BlockSpec((N_DIR, tm, RWKV_WIDTH), lambda i: (0, i, 0)),
                  row(RWKV_WIDTH), row(RWKV_WIDTH), row(CONV_WIDTH), row(3 * D_MODEL),
                  _const_spec((RWKV_WIDTH, RWKV_WIDTH)), _const_spec((1, RWKV_WIDTH)), _const_spec((1, RWKV_WIDTH)),
                  _const_spec((FNET_WIDTH, D_MODEL)), _const_spec((RWKV_WIDTH, D_MODEL)),
                  _const_spec((CONV_WIDTH, D_MODEL)), _const_spec((D_MODEL, D_MODEL))],
        out_specs=row(D_MODEL),
        out_shape=jax.ShapeDtypeStruct((t, D_MODEL), F32),
        compiler_params=_cparams(("parallel",)),
        name="merge",
    )(x, ya, o, bonus, g, yc, gates, _head_ones(), vec(p['rwkv_gn_w'][l]), vec(p['rwkv_gn_b'][l]),
      p['fnet_w'][l].astype(BF), p['rwkv_w_o'][l].astype(BF), p['conv_w_o'][l].astype(BF),
      p['mix_w_out'][l].astype(BF))


FFN_HALO = 16


def _erf(x):
    return lax.erf(x)


def _ffn_kernel(tiles_per_seq, final, x_ref, xp_ref, xn_ref, g_ref, wup_ref, dww_ref, dwb_ref, wdn_ref, gf_ref,
                out_ref, xn_s, h_s):
    tm = x_ref.shape[0]
    i = pl.program_id(0)
    first = (i % tiles_per_seq) == 0
    last = (i % tiles_per_seq) == tiles_per_seq - 1
    g = g_ref[...]
    hl = FFN_HALO
    xn_s[0:hl] = _rms(xp_ref[...], g).astype(BF)
    xn_s[hl:tm + hl] = _rms(x_ref[...], g).astype(BF)
    xn_s[tm + hl:tm + 2 * hl] = _rms(xn_ref[...], g).astype(BF)
    rows = lax.broadcasted_iota(jnp.int32, (tm + 2 * hl, 1), 0)
    pad = (first & (rows < hl)) | (last & (rows >= tm + hl))
    acc = jnp.zeros((tm, D_MODEL), F32)
    for lo, hi in FF_BLOCKS:
        h = jnp.dot(xn_s[...], wup_ref[:, lo:hi], preferred_element_type=F32)
        h_s[:, 0:hi - lo] = jnp.where(pad, 0.0, h)
        hc = (dww_ref[0:1, lo:hi] * h_s[hl - 1:tm + hl - 1, 0:hi - lo]
              + dww_ref[1:2, lo:hi] * h_s[hl:tm + hl, 0:hi - lo]
              + dww_ref[2:3, lo:hi] * h_s[hl + 1:tm + hl + 1, 0:hi - lo] + dwb_ref[:, lo:hi])
        gate = jnp.dot(xn_s[hl:tm + hl], wup_ref[:, D_FF + lo:D_FF + hi], preferred_element_type=F32)
        act = 0.5 * hc * (1.0 + _erf(hc * (1.0 / math.sqrt(2.0)))) * gate
        acc = acc + jnp.dot(act.astype(BF), wdn_ref[lo:hi, :], preferred_element_type=F32)
    y = x_ref[...] + acc
    if final:
        y = _rms(y, gf_ref[...])
    out_ref[...] = y


def ffn(x, seq, p, l, final):
    t = x.shape[0]
    tm = TOKEN_TILE
    hl = FFN_HALO
    nbh = tm // hl
    row = pl.BlockSpec((tm, D_MODEL), lambda i: (i, 0))
    vec = lambda a: a.reshape(1, -1).astype(F32)
    fb = max(hi - lo for lo, hi in FF_BLOCKS)
    return pl.pallas_call(
        functools.partial(_ffn_kernel, seq // tm, final),
        grid=(t // tm,),
        in_specs=[row,
                  pl.BlockSpec((hl, D_MODEL), lambda i: (jnp.maximum(i * nbh - 1, 0), 0)),
                  pl.BlockSpec((hl, D_MODEL), lambda i: (jnp.minimum((i + 1) * nbh, t // hl - 1), 0)),
                  _const_spec((1, D_MODEL)), _const_spec((D_MODEL, 2 * D_FF)),
                  _const_spec((3, D_FF)), _const_spec((1, D_FF)), _const_spec((D_FF, D_MODEL)),
                  _const_spec((1, D_MODEL))],
        out_specs=row,
        out_shape=jax.ShapeDtypeStruct((t, D_MODEL), F32),
        scratch_shapes=[pltpu.VMEM((tm + 2 * hl, D_MODEL), BF), pltpu.VMEM((tm + 2 * hl, fb), F32)],
        compiler_params=_cparams(("parallel",)),
        name="ffn",
    )(x, x, x, vec(p['ffn_norm_g'][l]), p['ffn_w_up'][l].astype(BF), p['ffn_dw_w'][l],
      vec(p['ffn_dw_b'][l]), p['ffn_w_down'][l].astype(BF), vec(p['final_norm_g']))


def encoder(x3, p):
    bsz, seq, _ = x3.shape
    depth = p['w_in'].shape[0]
    x = x3.reshape(bsz * seq, D_MODEL)
    tables = _fft_tables(seq)
    for l in range(depth):
        ua, ub, hc, gates = mixer_in(x, p['attn_norm_g'][l], p['w_in'][l].astype(BF))
        ya = fourier_mix(ua, bsz, seq, tables)
        r, kk, v, krep, kka, logw, bonus, g = rwkv_prep(ub, seq, p, l)
        o = rwkv_scan(r, kk, v, krep, kka, logw, bsz, seq)
        yc = conformer_conv(hc, seq, p, l)
        x = merge(x, ya, o, bonus, g, yc, gates, p, l)
        x = ffn(x, seq, p, l, final=(l == depth - 1))
    return x.reshape(bsz, seq, D_MODEL)


def kernel(x_prompt, x_sample, attn_norm_g, w_in, fnet_w, rwkv_mu_prev, rwkv_mu_next, rwkv_w0, rwkv_w_up,
           rwkv_a0, rwkv_a_up, rwkv_g_up, rwkv_k_k, rwkv_k_a, rwkv_r_k, rwkv_gn_w, rwkv_gn_b, rwkv_w_o,
           conv_dw_w, conv_dw_b, conv_ln_w, conv_ln_b, conv_w_o, mix_w_out, ffn_norm_g, ffn_w_up,
           ffn_dw_w, ffn_dw_b, ffn_w_down, final_norm_g):
    p = dict(attn_norm_g=attn_norm_g, w_in=w_in, fnet_w=fnet_w, rwkv_mu_prev=rwkv_mu_prev,
             rwkv_mu_next=rwkv_mu_next, rwkv_w0=rwkv_w0, rwkv_w_up=rwkv_w_up, rwkv_a0=rwkv_a0,
             rwkv_a_up=rwkv_a_up, rwkv_g_up=rwkv_g_up, rwkv_k_k=rwkv_k_k, rwkv_k_a=rwkv_k_a, rwkv_r_k=rwkv_r_k,
             rwkv_gn_w=rwkv_gn_w, rwkv_gn_b=rwkv_gn_b, rwkv_w_o=rwkv_w_o, conv_dw_w=conv_dw_w,
             conv_dw_b=conv_dw_b, conv_ln_w=conv_ln_w, conv_ln_b=conv_ln_b, conv_w_o=conv_w_o,
             mix_w_out=mix_w_out, ffn_norm_g=ffn_norm_g, ffn_w_up=ffn_w_up, ffn_dw_w=ffn_dw_w,
             ffn_dw_b=ffn_dw_b, ffn_w_down=ffn_w_down, final_norm_g=final_norm_g)
    return (encoder(x_prompt, p), encoder(x_sample, p))
```

```python
import functools
import math

import numpy as np
import jax
import jax.numpy as jnp
from jax import lax
from jax.experimental import pallas as pl
from jax.experimental.pallas import tpu as pltpu

D_MODEL = 1024
FNET_GROUPS = 4
FNET_GROUP_DIM = 64
FNET_WIDTH = 256
RWKV_HEADS = 8
RWKV_HEAD_DIM = 64
RWKV_WIDTH = 512
N_DIR = 2
W_LORA = 64
A_LORA = 64
G_LORA = 128
RWKV_IN = 1920
CONV_WIDTH = 256
CONV_KERNEL = 31
IN_COLS = 5760
D_FF = 2816
RMS_EPS = 1e-6
LN_EPS = 1e-5
GN_EPS = 64e-5
DECAY_SCALE = math.exp(-0.5)

COL_A = 0
COL_B = FNET_WIDTH
COL_C = COL_B + RWKV_IN
COL_G = COL_C + 2 * CONV_WIDTH

BF = jnp.bfloat16
F32 = jnp.float32

TOKEN_TILE = 512
SCAN_TILE = 512
CHUNK = 64
FFT_N2 = 64
FFT_COLS = 2048
FFT_K1_BLOCK = 8
VMEM_LIMIT = 56 * 1024 * 1024
FF_BLOCKS = ((0, 1024), (1024, 2048), (2048, 2816))


def _cparams(sem):
    return pltpu.CompilerParams(dimension_semantics=sem, vmem_limit_bytes=VMEM_LIMIT)


def _bdot(a, b):
    return jnp.dot(a.astype(BF), b.astype(BF), preferred_element_type=F32)


def _split_dot(a, b_exact):
    ah = a.astype(BF)
    al = (a - ah.astype(F32)).astype(BF)
    return (jnp.dot(ah, b_exact, preferred_element_type=F32)
            + jnp.dot(al, b_exact, preferred_element_type=F32))


def _dot_t0(a, b):
    return lax.dot_general(a, b, (((0,), (0,)), ((), ())), preferred_element_type=F32)


def _dot_t1(a, b):
    return lax.dot_general(a, b, (((1,), (1,)), ((), ())), preferred_element_type=F32)


def _sigmoid(x):
    return 1.0 / (1.0 + jnp.exp(-x))


def _rms(x, g):
    return x * lax.rsqrt(jnp.mean(x * x, axis=-1, keepdims=True) + RMS_EPS) * g


def _const_spec(shape):
    nd = len(shape)
    return pl.BlockSpec(shape, lambda *_: (0,) * nd)


def _mixer_in_kernel(x_ref, g_ref, w_ref, ua_ref, ub_ref, hc_ref, gate_ref):
    xn = _rms(x_ref[...], g_ref[...]).astype(BF)
    ua_ref[...] = jnp.dot(xn, w_ref[:, COL_A:COL_B], preferred_element_type=F32)
    for lo, hi in ((0, 1024), (1024, RWKV_IN)):
        ub_ref[:, lo:hi] = jnp.dot(xn, w_ref[:, COL_B + lo:COL_B + hi], preferred_element_type=F32)
    uc = jnp.dot(xn, w_ref[:, COL_C:COL_G], preferred_element_type=F32)
    hc_ref[...] = uc[:, :CONV_WIDTH] * _sigmoid(uc[:, CONV_WIDTH:])
    for j in range(3):
        lo = j * D_MODEL
        ug = jnp.dot(xn, w_ref[:, COL_G + lo:COL_G + lo + D_MODEL], preferred_element_type=F32)
        gate_ref[:, lo:lo + D_MODEL] = _sigmoid(ug).astype(BF)


def mixer_in(x, g, w_in_bf):
    t = x.shape[0]
    tm = TOKEN_TILE
    row = lambda w: pl.BlockSpec((tm, w), lambda i: (i, 0))
    return pl.pallas_call(
        _mixer_in_kernel,
        grid=(t // tm,),
        in_specs=[row(D_MODEL), _const_spec((1, D_MODEL)), _const_spec((D_MODEL, IN_COLS))],
        out_specs=[row(FNET_WIDTH), row(RWKV_IN), row(CONV_WIDTH), row(3 * D_MODEL)],
        out_shape=[jax.ShapeDtypeStruct((t, FNET_WIDTH), F32), jax.ShapeDtypeStruct((t, RWKV_IN), F32),
                   jax.ShapeDtypeStruct((t, CONV_WIDTH), F32), jax.ShapeDtypeStruct((t, 3 * D_MODEL), BF)],
        compiler_params=_cparams(("parallel",)),
        name="mixer_in",
    )(x, g.reshape(1, D_MODEL), w_in_bf)


def _fft_tables(seq):
    n2 = FFT_N2
    n1 = seq // n2
    k1 = np.arange(n1)[:, None].astype(np.float64)
    m1 = np.arange(n1)[None, :].astype(np.float64)
    ang1 = 2.0 * np.pi * ((k1 * m1) % n1) / n1
    f1 = np.concatenate([np.cos(ang1), -np.sin(ang1)], axis=0)
    m2 = np.arange(n2)[None, :].astype(np.float64)
    angt = 2.0 * np.pi * ((k1 * m2) % seq) / seq
    tr, ti = np.cos(angt), -np.sin(angt)
    k2 = np.arange(n2)[:, None].astype(np.float64)
    ang2 = 2.0 * np.pi * ((k2 * m2) % n2) / n2
    c2, s2 = np.cos(ang2), np.sin(ang2)
    f2 = np.block([[c2, s2], [-s2, c2]])
    q = np.arange(FNET_GROUP_DIM)
    angc = 2.0 * np.pi * ((q[:, None] * q[None, :]) % FNET_GROUP_DIM) / FNET_GROUP_DIM
    scale = 1.0 / math.sqrt(seq * FNET_GROUP_DIM)
    eye = np.eye(FNET_GROUPS)
    cd = np.concatenate([np.kron(eye, np.cos(angc)), np.kron(eye, np.sin(angc))], axis=0) * scale
    tr = jnp.repeat(jnp.asarray(tr, F32), FNET_WIDTH, axis=1)
    ti = jnp.repeat(jnp.asarray(ti, F32), FNET_WIDTH, axis=1)
    return (jnp.asarray(f1, BF), tr, ti, jnp.asarray(f2, BF), jnp.asarray(cd, BF))


def _fft1_kernel(x_ref, f1_ref, tr_ref, ti_ref, o_ref):
    n1 = x_ref.shape[1]
    res = jnp.dot(f1_ref[...], x_ref[0].astype(BF), preferred_element_type=F32)
    ar, ai = res[:n1], res[n1:]
    tr, ti = tr_ref[...], ti_ref[...]
    o_ref[0, 0] = ar * tr - ai * ti
    o_ref[0, 1] = ar * ti + ai * tr


def _fft2_kernel(z_ref, f2_ref, cd_ref, o_ref):
    n2 = FFT_N2
    for i in range(FFT_K1_BLOCK):
        z = jnp.concatenate([z_ref[0, 0, i], z_ref[0, 1, i]], axis=0).astype(BF)
        g = jnp.dot(f2_ref[...], z, preferred_element_type=F32)
        y = (jnp.dot(g[:n2].astype(BF), cd_ref[:FNET_WIDTH], preferred_element_type=F32)
             + jnp.dot(g[n2:].astype(BF), cd_ref[FNET_WIDTH:], preferred_element_type=F32))
        o_ref[0, :, i * FNET_WIDTH:(i + 1) * FNET_WIDTH] = y


def fourier_mix(ua, bsz, seq, tables):
    f1, tr, ti, f2, cd = tables
    n2 = FFT_N2
    n1 = seq // n2
    cols = n2 * FNET_WIDTH
    cb = FFT_COLS
    x = ua.reshape(bsz, n1, cols)
    z = pl.pallas_call(
        _fft1_kernel,
        grid=(bsz, cols // cb),
        in_specs=[pl.BlockSpec((1, n1, cb), lambda b, j: (b, 0, j)),
                  _const_spec((2 * n1, n1)),
                  pl.BlockSpec((n1, cb), lambda b, j: (0, j)),
                  pl.BlockSpec((n1, cb), lambda b, j: (0, j))],
        out_specs=pl.BlockSpec((1, 2, n1, cb), lambda b, j: (b, 0, 0, j)),
        out_shape=jax.ShapeDtypeStruct((bsz, 2, n1, cols), F32),
        compiler_params=_cparams(("parallel", "parallel")),
        name="fft1",
    )(x, f1, tr, ti)
    z = z.reshape(bsz, 2, n1, n2, FNET_WIDTH)
    kb = FFT_K1_BLOCK
    y = pl.pallas_call(
        _fft2_kernel,
        grid=(bsz, n1 // kb),
        in_specs=[pl.BlockSpec((1, 2, kb, n2, FNET_WIDTH), lambda b, j: (b, 0, j, 0, 0)),
                  _const_spec((2 * n2, 2 * n2)),
                  _const_spec((2 * FNET_WIDTH, FNET_WIDTH))],
        out_specs=pl.BlockSpec((1, n2, kb * FNET_WIDTH), lambda b, j: (b, 0, j)),
        out_shape=jax.ShapeDtypeStruct((bsz, n2, n1 * FNET_WIDTH), F32),
        compiler_params=_cparams(("parallel", "parallel")),
        name="fft2",
    )(z, f2, cd)
    return y.reshape(bsz * seq, FNET_WIDTH)


def _head_ones():
    h = np.arange(RWKV_WIDTH) // RWKV_HEAD_DIM
    return jnp.asarray((h[:, None] == h[None, :]).astype(np.float32), BF)


def _rwkv_prep_kernel(tiles_per_seq, u_ref, up_ref, un_ref, mup_ref, mun_ref, w0_ref, wup_ref, a0_ref, aup_ref,
                      gup_ref, kk_w_ref, ka_ref, rk_ref, ones_ref,
                      r_ref, kkn_ref, v_ref, krep_ref, kka_ref, logw_ref, bonus_ref, g_ref, buf):
    tm = u_ref.shape[0]
    i = pl.program_id(0)
    first = (i % tiles_per_seq) == 0
    last = (i % tiles_per_seq) == tiles_per_seq - 1
    buf[0:8] = jnp.where(first, 0.0, up_ref[...])
    buf[8:tm + 8] = u_ref[...]
    buf[tm + 8:tm + 16] = jnp.where(last, 0.0, un_ref[...])
    mup, mun = mup_ref[...], mun_ref[...]

    def shifted(lo, hi):
        u = buf[8:tm + 8, lo:hi]
        return (u + mup[:, lo:hi] * (buf[7:tm + 7, lo:hi] - u) + mun[:, lo:hi] * (buf[9:tm + 9, lo:hi] - u))

    c0 = RWKV_WIDTH
    r = shifted(0, c0)
    k = shifted(c0, 2 * c0)
    v = shifted(2 * c0, 3 * c0)
    wd = shifted(3 * c0, 3 * c0 + 128)
    ad = shifted(3 * c0 + 128, 3 * c0 + 256)
    gd = shifted(3 * c0 + 256, 3 * c0 + 384)

    ones = ones_ref[...]
    r_ref[...] = r
    v_ref[...] = v
    kk = k * kk_w_ref[...]
    ss = _split_dot(kk * kk, ones)
    kk = kk * lax.rsqrt(jnp.maximum(ss, 1e-24))
    kkn_ref[...] = kk
    bonus_ref[...] = _split_dot(r * k * rk_ref[...], ones) * v
    g_ref[...] = _bdot(_sigmoid(gd), gup_ref[...])

    wlogit = _bdot(jnp.tanh(wd), wup_ref[...])
    alogit = _bdot(ad, aup_ref[...])
    ka = ka_ref[...]
    for d in range(N_DIR):
        sl = slice(d * c0, (d + 1) * c0)
        logw_ref[d] = -DECAY_SCALE * _sigmoid(w0_ref[:, sl] + wlogit[:, sl])
        a = _sigmoid(a0_ref[:, sl] + alogit[:, sl])
        krep_ref[d] = k * (1.0 + (a - 1.0) * ka)
        kka_ref[d] = kk * a


def _lora_block(w):
    z = jnp.zeros_like(w[0])
    return jnp.concatenate([jnp.concatenate([w[0], z], axis=1), jnp.concatenate([z, w[1]], axis=1)], axis=0)


def rwkv_prep(ub, seq, p, l):
    t = ub.shape[0]
    tm = TOKEN_TILE
    nb8 = tm // 8
    row = lambda w: pl.BlockSpec((tm, w), lambda i: (i, 0))
    row2 = pl.BlockSpec((N_DIR, tm, RWKV_WIDTH), lambda i: (0, i, 0))
    vec = lambda a: a.reshape(1, -1).astype(F32)
    sds = jax.ShapeDtypeStruct
    outs = pl.pallas_call(
        functools.partial(_rwkv_prep_kernel, seq // tm),
        grid=(t // tm,),
        in_specs=[row(RWKV_IN),
                  pl.BlockSpec((8, RWKV_IN), lambda i: (jnp.maximum(i * nb8 - 1, 0), 0)),
                  pl.BlockSpec((8, RWKV_IN), lambda i: (jnp.minimum((i + 1) * nb8, t // 8 - 1), 0)),
                  _const_spec((1, RWKV_IN)), _const_spec((1, RWKV_IN)),
                  _const_spec((1, 2 * RWKV_WIDTH)), _const_spec((2 * W_LORA, 2 * RWKV_WIDTH)),
                  _const_spec((1, 2 * RWKV_WIDTH)), _const_spec((2 * A_LORA, 2 * RWKV_WIDTH)),
                  _const_spec((G_LORA, RWKV_WIDTH)),
                  _const_spec((1, RWKV_WIDTH)), _const_spec((1, RWKV_WIDTH)), _const_spec((1, RWKV_WIDTH)),
                  _const_spec((RWKV_WIDTH, RWKV_WIDTH))],
        out_specs=[row(RWKV_WIDTH), row(RWKV_WIDTH), row(RWKV_WIDTH), row2, row2, row2,
                   row(RWKV_WIDTH), row(RWKV_WIDTH)],
        out_shape=[sds((t, RWKV_WIDTH), F32)] * 3 + [sds((N_DIR, t, RWKV_WIDTH), F32)] * 3
                  + [sds((t, RWKV_WIDTH), F32)] * 2,
        scratch_shapes=[pltpu.VMEM((tm + 16, RWKV_IN), F32)],
        compiler_params=_cparams(("parallel",)),
        name="rwkv_prep",
    )(ub, ub, ub, vec(p['rwkv_mu_prev'][l]), vec(p['rwkv_mu_next'][l]),
      vec(p['rwkv_w0'][l]), _lora_block(p['rwkv_w_up'][l]).astype(BF),
      vec(p['rwkv_a0'][l]), _lora_block(p['rwkv_a_up'][l]).astype(BF),
      p['rwkv_g_up'][l].astype(BF), vec(p['rwkv_k_k'][l]), vec(p['rwkv_k_a'][l]), vec(p['rwkv_r_k'][l]),
      _head_ones())
    return outs


GROUP_HEADS = 4
GROUP = GROUP_HEADS * RWKV_HEAD_DIM
N_GROUP = RWKV_HEADS // GROUP_HEADS
SCAN_CHUNKS_PER_ITER = 4


def _scan_kernel(r_ref, kk_ref, v_ref, krep_ref, kka_ref, logw_ref, o_ref,
                 s_ref, ar_s, t_s, mrb_s, x0_s, o0_s, bg_s, kv_s, dec_s):
    L = CHUNK
    nch = r_ref.shape[0] // L
    cpi = SCAN_CHUNKS_PER_ITER
    d = pl.program_id(1)

    @pl.when(pl.program_id(2) == 0)
    def _():
        s_ref[...] = jnp.zeros_like(s_ref)

    sgn = 1 - 2 * d
    r64 = lax.broadcasted_iota(jnp.int32, (L, L), 0)
    c64 = lax.broadcasted_iota(jnp.int32, (L, L), 1)
    tri = jnp.where((r64 - c64) * sgn >= 0, 1.0, 0.0).astype(BF)
    row = lax.broadcasted_iota(jnp.int32, (L, GROUP), 0)
    sidx = lax.broadcasted_iota(jnp.int32, (L, GROUP), 1) % L
    diff = (row - sidx) * sgn
    strict = diff > 0
    incl = diff >= 0
    eye = jnp.where(row == sidx, 1.0, 0.0)
    brow = lax.broadcasted_iota(jnp.int32, (GROUP, GROUP), 0) // L
    bcol = lax.broadcasted_iota(jnp.int32, (GROUP, GROUP), 1) // L
    same_head = brow == bcol
    bd_ones = jnp.where(same_head, 1.0, 0.0).astype(BF)

    def bdiag(x):
        xb = x.astype(BF)
        return jnp.concatenate([xb] * GROUP_HEADS, axis=0) * bd_ones

    def off_mask(b):
        return ((row // (2 * b)) == (sidx // (2 * b))) & ((row // b) != (sidx // b))

    def phase1_body(jb, carry):
        chains = []
        for q in range(cpi):
            j = jb * cpi + q
            sl = pl.ds(pl.multiple_of(j * L, L), L)
            lw = logw_ref[0, sl, :]
            lwh = lw.astype(BF)
            lwl = (lw - lwh.astype(F32)).astype(BF)
            g = jnp.dot(tri, lwh, preferred_element_type=F32) + jnp.dot(tri, lwl, preferred_element_type=F32)
            gtot = jnp.sum(lw, axis=0, keepdims=True)
            e_e = jnp.exp(g - lw)
            e_mg = jnp.exp(-g)
            e_gt = jnp.exp(gtot - g)
            kk = kk_ref[sl, :]
            kka = kka_ref[0, sl, :]
            krep = krep_ref[0, sl, :]
            at = kk * e_e
            bh = kka * e_mg
            kh = krep * e_mg
            rh = r_ref[sl, :] * jnp.exp(g)
            bg = kka * e_gt
            kg = krep * e_gt
            vv = v_ref[sl, :]
            dec_s[j] = jnp.broadcast_to(jnp.exp(gtot), (8, RWKV_WIDTH))
            for gi in range(N_GROUP):
                gs = slice(gi * GROUP, (gi + 1) * GROUP)
                ar = jnp.concatenate([at[:, gs], rh[:, gs]], axis=0).astype(BF)
                ar_s[j, gi] = ar
                bg_s[j, gi] = bg[:, gs].astype(BF)
                chains.append(dict(j=j, gi=gi, ar=ar, bh=bh[:, gs], kh=kh[:, gs], v=vv[:, gs], kg=kg[:, gs]))
        for c in chains:
            c['nb'] = _dot_t1(c['ar'], bdiag(c['bh']))
        for c in chains:
            c['nk'] = _dot_t1(c['ar'], bdiag(c['kh']))
        for c in chains:
            nk = c['nk']
            lhs = jnp.concatenate([jnp.where(strict, nk[:L], 0.0), jnp.where(incl, nk[L:], 0.0)], axis=0)
            xo = jnp.dot(lhs.astype(BF), bdiag(c['v']), preferred_element_type=F32)
            x0_s[c['j'], c['gi']] = xo[:L]
            o0_s[c['j'], c['gi']] = xo[L:]
        for c in chains:
            kv = _dot_t0(c['v'].astype(BF), c['kg'].astype(BF))
            kv_s[c['j'], c['gi']] = jnp.where(same_head, kv, 0.0)
        for c in chains:
            nb = c['nb']
            mrb_s[c['j'], c['gi']] = jnp.where(incl, nb[L:], 0.0).astype(BF)
            c['n'] = jnp.where(strict, nb[:L], 0.0)
            c['t'] = eye - jnp.where(off_mask(1), c['n'], 0.0)
        b = 2
        while b < L:
            om = off_mask(b)
            for c in chains:
                c['p'] = jnp.dot(c['t'].astype(BF), bdiag(jnp.where(om, c['n'], 0.0)),
                                 preferred_element_type=F32)
            for c in chains:
                c['t'] = c['t'] - jnp.dot(c['p'].astype(BF), bdiag(c['t']), preferred_element_type=F32)
            b *= 2
        for c in chains:
            t_s[c['j'], c['gi']] = c['t'].astype(BF)
        return carry

    lax.fori_loop(0, nch // cpi, phase1_body, 0)

    def phase2_body(jj, carry):
        j = jj + d * (nch - 1 - 2 * jj)
        sl = pl.ds(pl.multiple_of(j * L, L), L)
        groups = range(N_GROUP)
        s = [s_ref[gi] for gi in groups]
        arh = [_dot_t1(ar_s[j, gi], s[gi].astype(BF)) for gi in groups]
        u = [jnp.dot(t_s[j, gi], bdiag(arh[gi][:L] + x0_s[j, gi]), preferred_element_type=F32) for gi in groups]
        bu = [_dot_t0(u[gi].astype(BF), bg_s[j, gi]) for gi in groups]
        dec = dec_s[j]
        for gi in groups:
            gs = slice(gi * GROUP, (gi + 1) * GROUP)
            s_ref[gi] = s[gi] * dec[0:1, gs] + kv_s[j, gi] - jnp.where(same_head, bu[gi], 0.0)
        for gi in groups:
            gs = slice(gi * GROUP, (gi + 1) * GROUP)
            o_ref[0, sl, gs] = (arh[gi][L:] + o0_s[j, gi]
                                - jnp.dot(mrb_s[j, gi], bdiag(u[gi]), preferred_element_type=F32))
        return carry

    lax.fori_loop(0, nch, phase2_body, 0)


def rwkv_scan(r, kk, v, krep, kka, logw, bsz, seq):
    t = r.shape[0]
    ts = SCAN_TILE
    nc = seq // ts
    nch = ts // CHUNK

    def tblk(b, d, c):
        return b * nc + c + d * (nc - 1 - 2 * c)

    shared = pl.BlockSpec((ts, RWKV_WIDTH), lambda b, d, c: (tblk(b, d, c), 0))
    perdir = pl.BlockSpec((1, ts, RWKV_WIDTH), lambda b, d, c: (d, tblk(b, d, c), 0))
    return pl.pallas_call(
        _scan_kernel,
        grid=(bsz, N_DIR, nc),
        in_specs=[shared, shared, shared, perdir, perdir, perdir],
        out_specs=perdir,
        out_shape=jax.ShapeDtypeStruct((N_DIR, t, RWKV_WIDTH), F32),
        scratch_shapes=[pltpu.VMEM((N_GROUP, GROUP, GROUP), F32),
                        pltpu.VMEM((nch, N_GROUP, 2 * CHUNK, GROUP), BF),
                        pltpu.VMEM((nch, N_GROUP, CHUNK, GROUP), BF),
                        pltpu.VMEM((nch, N_GROUP, CHUNK, GROUP), BF),
                        pltpu.VMEM((nch, N_GROUP, CHUNK, GROUP), F32),
                        pltpu.VMEM((nch, N_GROUP, CHUNK, GROUP), F32),
                        pltpu.VMEM((nch, N_GROUP, CHUNK, GROUP), BF),
                        pltpu.VMEM((nch, N_GROUP, GROUP, GROUP), F32),
                        pltpu.VMEM((nch, 8, RWKV_WIDTH), F32)],
        compiler_params=_cparams(("parallel", "arbitrary", "arbitrary")),
        name="rwkv_scan",
    )(r, kk, v, krep, kka, logw)


CONV_HALO = 16


def _conv_kernel(tiles_per_seq, h_ref, hp_ref, hn_ref, w_ref, b_ref, lnw_ref, lnb_ref, o_ref, buf):
    tm = h_ref.shape[0]
    i = pl.program_id(0)
    first = (i % tiles_per_seq) == 0
    last = (i % tiles_per_seq) == tiles_per_seq - 1
    buf[0:CONV_HALO] = jnp.where(first, 0.0, hp_ref[...])
    buf[CONV_HALO:tm + CONV_HALO] = h_ref[...]
    buf[tm + CONV_HALO:tm + 2 * CONV_HALO] = jnp.where(last, 0.0, hn_ref[...])
    acc = jnp.zeros((tm, CONV_WIDTH), F32) + b_ref[...]
    for k in range(CONV_KERNEL):
        off = CONV_HALO - CONV_KERNEL // 2 + k
        acc = acc + w_ref[k:k + 1, :] * buf[off:off + tm, :]
    mu = jnp.mean(acc, axis=-1, keepdims=True)
    xc = acc - mu
    var = jnp.mean(xc * xc, axis=-1, keepdims=True)
    y = xc * lax.rsqrt(var + LN_EPS) * lnw_ref[...] + lnb_ref[...]
    o_ref[...] = y * _sigmoid(y)


def conformer_conv(hc, seq, p, l):
    t = hc.shape[0]
    tm = TOKEN_TILE
    nbh = tm // CONV_HALO
    row = pl.BlockSpec((tm, CONV_WIDTH), lambda i: (i, 0))
    vec = lambda a: a.reshape(1, -1).astype(F32)
    return pl.pallas_call(
        functools.partial(_conv_kernel, seq // tm),
        grid=(t // tm,),
        in_specs=[row,
                  pl.BlockSpec((CONV_HALO, CONV_WIDTH), lambda i: (jnp.maximum(i * nbh - 1, 0), 0)),
                  pl.BlockSpec((CONV_HALO, CONV_WIDTH),
                               lambda i: (jnp.minimum((i + 1) * nbh, t // CONV_HALO - 1), 0)),
                  _const_spec((CONV_KERNEL, CONV_WIDTH)), _const_spec((1, CONV_WIDTH)),
                  _const_spec((1, CONV_WIDTH)), _const_spec((1, CONV_WIDTH))],
        out_specs=row,
        out_shape=jax.ShapeDtypeStruct((t, CONV_WIDTH), F32),
        scratch_shapes=[pltpu.VMEM((tm + 2 * CONV_HALO, CONV_WIDTH), F32)],
        compiler_params=_cparams(("parallel",)),
        name="conformer_conv",
    )(hc, hc, hc, p['conv_dw_w'][l], vec(p['conv_dw_b'][l]), vec(p['conv_ln_w'][l]), vec(p['conv_ln_b'][l]))


def _merge_kernel(x_ref, ya_ref, o_ref, bonus_ref, g_ref, yc_ref, gate_ref, ones_ref, gnw_ref, gnb_ref,
                  wa_ref, wb_ref, wc_ref, wo_ref, out_ref):
    ones = ones_ref[...]
    o = o_ref[0] + o_ref[1]
    inv = 1.0 / RWKV_HEAD_DIM
    mu = _split_dot(o, ones) * inv
    oc = o - mu
    var = _split_dot(oc * oc, ones) * inv
    on = oc * lax.rsqrt(var + GN_EPS) * gnw_ref[...] + gnb_ref[...]
    yb = (on + bonus_ref[...]) * g_ref[...]
    m = gate_ref[:, 0:D_MODEL].astype(F32) * _bdot(ya_ref[...], wa_ref[...])
    m = m + gate_ref[:, D_MODEL:2 * D_MODEL].astype(F32) * _bdot(yb, wb_ref[...])
    m = m + gate_ref[:, 2 * D_MODEL:3 * D_MODEL].astype(F32) * _bdot(yc_ref[...], wc_ref[...])
    out_ref[...] = x_ref[...] + _bdot(m, wo_ref[...])


def merge(x, ya, o, bonus, g, yc, gates, p, l):
    t = x.shape[0]
    tm = TOKEN_TILE
    row = lambda w: pl.BlockSpec((tm, w), lambda i: (i, 0))
    vec = lambda a: a.reshape(1, -1).astype(F32)
    return pl.pallas_call(
        _merge_kernel,
        grid=(t // tm,),
        in_specs=[row(D_MODEL), row(FNET_WIDTH),
                  pl.BlockSpec((N_DIR, tm, RWKV_WIDTH), lambda i: (0, i, 0)),
                  row(RWKV_WIDTH), row(RWKV_WIDTH), row(CONV_WIDTH), row(3 * D_MODEL),
                  _const_spec((RWKV_WIDTH, RWKV_WIDTH)), _const_spec((1, RWKV_WIDTH)), _const_spec((1, RWKV_WIDTH)),
                  _const_spec((FNET_WIDTH, D_MODEL)), _const_spec((RWKV_WIDTH, D_MODEL)),
                  _const_spec((CONV_WIDTH, D_MODEL)), _const_spec((D_MODEL, D_MODEL))],
        out_specs=row(D_MODEL),
        out_shape=jax.ShapeDtypeStruct((t, D_MODEL), F32),
        compiler_params=_cparams(("parallel",)),
        name="merge",
    )(x, ya, o, bonus, g, yc, gates, _head_ones(), vec(p['rwkv_gn_w'][l]), vec(p['rwkv_gn_b'][l]),
      p['fnet_w'][l].astype(BF), p['rwkv_w_o'][l].astype(BF), p['conv_w_o'][l].astype(BF),
      p['mix_w_out'][l].astype(BF))


FFN_HALO = 16


def _erf(x):
    return lax.erf(x)


def _ffn_kernel(tiles_per_seq, final, x_ref, xp_ref, xn_ref, g_ref, wup_ref, dww_ref, dwb_ref, wdn_ref, gf_ref,
                out_ref, xn_s, h_s):
    tm = x_ref.shape[0]
    i = pl.program_id(0)
    first = (i % tiles_per_seq) == 0
    last = (i % tiles_per_seq) == tiles_per_seq - 1
    g = g_ref[...]
    hl = FFN_HALO
    xn_s[0:hl] = _rms(xp_ref[...], g).astype(BF)
    xn_s[hl:tm + hl] = _rms(x_ref[...], g).astype(BF)
    xn_s[tm + hl:tm + 2 * hl] = _rms(xn_ref[...], g).astype(BF)
    rows = lax.broadcasted_iota(jnp.int32, (tm + 2 * hl, 1), 0)
    pad = (first & (rows < hl)) | (last & (rows >= tm + hl))
    acc = jnp.zeros((tm, D_MODEL), F32)
    for lo, hi in FF_BLOCKS:
        h = jnp.dot(xn_s[...], wup_ref[:, lo:hi], preferred_element_type=F32)
        h_s[:, 0:hi - lo] = jnp.where(pad, 0.0, h)
        hc = (dww_ref[0:1, lo:hi] * h_s[hl - 1:tm + hl - 1, 0:hi - lo]
              + dww_ref[1:2, lo:hi] * h_s[hl:tm + hl, 0:hi - lo]
              + dww_ref[2:3, lo:hi] * h_s[hl + 1:tm + hl + 1, 0:hi - lo] + dwb_ref[:, lo:hi])
        gate = jnp.dot(xn_s[hl:tm + hl], wup_ref[:, D_FF + lo:D_FF + hi], preferred_element_type=F32)
        act = 0.5 * hc * (1.0 + _erf(hc * (1.0 / math.sqrt(2.0)))) * gate
        acc = acc + jnp.dot(act.astype(BF), wdn_ref[lo:hi, :], preferred_element_type=F32)
    y = x_ref[...] + acc
    if final:
        y = _rms(y, gf_ref[...])
    out_ref[...] = y


def ffn(x, seq, p, l, final):
    t = x.shape[0]
    tm = TOKEN_TILE
    hl = FFN_HALO
    nbh = tm // hl
    row = pl.BlockSpec((tm, D_MODEL), lambda i: (i, 0))
    vec = lambda a: a.reshape(1, -1).astype(F32)
    fb = max(hi - lo for lo, hi in FF_BLOCKS)
    return pl.pallas_call(
        functools.partial(_ffn_kernel, seq // tm, final),
        grid=(t // tm,),
        in_specs=[row,
                  pl.BlockSpec((hl, D_MODEL), lambda i: (jnp.maximum(i * nbh - 1, 0), 0)),
                  pl.BlockSpec((hl, D_MODEL), lambda i: (jnp.minimum((i + 1) * nbh, t // hl - 1), 0)),
                  _const_spec((1, D_MODEL)), _const_spec((D_MODEL, 2 * D_FF)),
                  _const_spec((3, D_FF)), _const_spec((1, D_FF)), _const_spec((D_FF, D_MODEL)),
                  _const_spec((1, D_MODEL))],
        out_specs=row,
        out_shape=jax.ShapeDtypeStruct((t, D_MODEL), F32),
        scratch_shapes=[pltpu.VMEM((tm + 2 * hl, D_MODEL), BF), pltpu.VMEM((tm + 2 * hl, fb), F32)],
        compiler_params=_cparams(("parallel",)),
        name="ffn",
    )(x, x, x, vec(p['ffn_norm_g'][l]), p['ffn_w_up'][l].astype(BF), p['ffn_dw_w'][l],
      vec(p['ffn_dw_b'][l]), p['ffn_w_down'][l].astype(BF), vec(p['final_norm_g']))


def encoder(x3, p):
    bsz, seq, _ = x3.shape
    depth = p['w_in'].shape[0]
    x = x3.reshape(bsz * seq, D_MODEL)
    tables = _fft_tables(seq)
    for l in range(depth):
        ua, ub, hc, gates = mixer_in(x, p['attn_norm_g'][l], p['w_in'][l].astype(BF))
        ya = fourier_mix(ua, bsz, seq, tables)
        r, kk, v, krep, kka, logw, bonus, g = rwkv_prep(ub, seq, p, l)
        o = rwkv_scan(r, kk, v, krep, kka, logw, bsz, seq)
        yc = conformer_conv(hc, seq, p, l)
        x = merge(x, ya, o, bonus, g, yc, gates, p, l)
        x = ffn(x, seq, p, l, final=(l == depth - 1))
    return x.reshape(bsz, seq, D_MODEL)


def kernel(x_prompt, x_sample, attn_norm_g, w_in, fnet_w, rwkv_mu_prev, rwkv_mu_next, rwkv_w0, rwkv_w_up,
           rwkv_a0, rwkv_a_up, rwkv_g_up, rwkv_k_k, rwkv_k_a, rwkv_r_k, rwkv_gn_w, rwkv_gn_b, rwkv_w_o,
           conv_dw_w, conv_dw_b, conv_ln_w, conv_ln_b, conv_w_o, mix_w_out, ffn_norm_g, ffn_w_up,
           ffn_dw_w, ffn_dw_b, ffn_w_down, final_norm_g):
    p = dict(attn_norm_g=attn_norm_g, w_in=w_in, fnet_w=fnet_w, rwkv_mu_prev=rwkv_mu_prev,
             rwkv_mu_next=rwkv_mu_next, rwkv_w0=rwkv_w0, rwkv_w_up=rwkv_w_up, rwkv_a0=rwkv_a0,
             rwkv_a_up=rwkv_a_up, rwkv_g_up=rwkv_g_up, rwkv_k_k=rwkv_k_k, rwkv_k_a=rwkv_k_a, rwkv_r_k=rwkv_r_k,
             rwkv_gn_w=rwkv_gn_w, rwkv_gn_b=rwkv_gn_b, rwkv_w_o=rwkv_w_o, conv_dw_w=conv_dw_w,
             conv_dw_b=conv_dw_b, conv_ln_w=conv_ln_w, conv_ln_b=conv_ln_b, conv_w_o=conv_w_o,
             mix_w_out=mix_w_out, ffn_norm_g=ffn_norm_g, ffn_w_up=ffn_w_up, ffn_dw_w=ffn_dw_w,
             ffn_dw_b=ffn_dw_b, ffn_w_down=ffn_w_down, final_norm_g=final_norm_g)
    return (encoder(x_prompt, p), encoder(x_sample, p))
```

```python
import functools
import math

import numpy as np
import jax
import jax.numpy as jnp
from jax import lax
from jax.experimental import pallas as pl
from jax.experimental.pallas import tpu as pltpu

D_MODEL = 1024
FNET_GROUPS = 4
FNET_GROUP_DIM = 64
FNET_WIDTH = 256
RWKV_HEADS = 8
RWKV_HEAD_DIM = 64
RWKV_WIDTH = 512
N_DIR = 2
W_LORA = 64
A_LORA = 64
G_LORA = 128
RWKV_IN = 1920
CONV_WIDTH = 256
CONV_KERNEL = 31
IN_COLS = 5760
D_FF = 2816
RMS_EPS = 1e-6
LN_EPS = 1e-5
GN_EPS = 64e-5
DECAY_SCALE = math.exp(-0.5)

COL_A = 0
COL_B = FNET_WIDTH
COL_C = COL_B + RWKV_IN
COL_G = COL_C + 2 * CONV_WIDTH

BF = jnp.bfloat16
F32 = jnp.float32

TOKEN_TILE = 512
SCAN_TILE = 512
CHUNK = 64
FFT_N2 = 64
FFT_COLS = 2048
FFT_K1_BLOCK = 8
VMEM_LIMIT = 56 * 1024 * 1024
FF_BLOCKS = ((0, 1024), (1024, 2048), (2048, 2816))


def _cparams(sem):
    return pltpu.CompilerParams(dimension_semantics=sem, vmem_limit_bytes=VMEM_LIMIT)


def _bdot(a, b):
    return jnp.dot(a.astype(BF), b.astype(BF), preferred_element_type=F32)


def _split_dot(a, b_exact):
    ah = a.astype(BF)
    al = (a - ah.astype(F32)).astype(BF)
    return (jnp.dot(ah, b_exact, preferred_element_type=F32)
            + jnp.dot(al, b_exact, preferred_element_type=F32))


def _dot_t0(a, b):
    return lax.dot_general(a, b, (((0,), (0,)), ((), ())), preferred_element_type=F32)


def _dot_t1(a, b):
    return lax.dot_general(a, b, (((1,), (1,)), ((), ())), preferred_element_type=F32)


def _sigmoid(x):
    return 1.0 / (1.0 + jnp.exp(-x))


def _rms(x, g):
    return x * lax.rsqrt(jnp.mean(x * x, axis=-1, keepdims=True) + RMS_EPS) * g


def _const_spec(shape):
    nd = len(shape)
    return pl.BlockSpec(shape, lambda *_: (0,) * nd)


def _mixer_in_kernel(x_ref, g_ref, w_ref, ua_ref, ub_ref, hc_ref, gate_ref):
    xn = _rms(x_ref[...], g_ref[...]).astype(BF)
    ua_ref[...] = jnp.dot(xn, w_ref[:, COL_A:COL_B], preferred_element_type=F32)
    for lo, hi in ((0, 1024), (1024, RWKV_IN)):
        ub_ref[:, lo:hi] = jnp.dot(xn, w_ref[:, COL_B + lo:COL_B + hi], preferred_element_type=F32)
    uc = jnp.dot(xn, w_ref[:, COL_C:COL_G], preferred_element_type=F32)
    hc_ref[...] = uc[:, :CONV_WIDTH] * _sigmoid(uc[:, CONV_WIDTH:])
    for j in range(3):
        lo = j * D_MODEL
        ug = jnp.dot(xn, w_ref[:, COL_G + lo:COL_G + lo + D_MODEL], preferred_element_type=F32)
        gate_ref[:, lo:lo + D_MODEL] = _sigmoid(ug).astype(BF)


def mixer_in(x, g, w_in_bf):
    t = x.shape[0]
    tm = TOKEN_TILE
    row = lambda w: pl.BlockSpec((tm, w), lambda i: (i, 0))
    return pl.pallas_call(
        _mixer_in_kernel,
        grid=(t // tm,),
        in_specs=[row(D_MODEL), _const_spec((1, D_MODEL)), _const_spec((D_MODEL, IN_COLS))],
        out_specs=[row(FNET_WIDTH), row(RWKV_IN), row(CONV_WIDTH), row(3 * D_MODEL)],
        out_shape=[jax.ShapeDtypeStruct((t, FNET_WIDTH), F32), jax.ShapeDtypeStruct((t, RWKV_IN), F32),
                   jax.ShapeDtypeStruct((t, CONV_WIDTH), F32), jax.ShapeDtypeStruct((t, 3 * D_MODEL), BF)],
        compiler_params=_cparams(("parallel",)),
        name="mixer_in",
    )(x, g.reshape(1, D_MODEL), w_in_bf)


def _fft_tables(seq):
    n2 = FFT_N2
    n1 = seq // n2
    k1 = np.arange(n1)[:, None].astype(np.float64)
    m1 = np.arange(n1)[None, :].astype(np.float64)
    ang1 = 2.0 * np.pi * ((k1 * m1) % n1) / n1
    f1 = np.concatenate([np.cos(ang1), -np.sin(ang1)], axis=0)
    m2 = np.arange(n2)[None, :].astype(np.float64)
    angt = 2.0 * np.pi * ((k1 * m2) % seq) / seq
    tr, ti = np.cos(angt), -np.sin(angt)
    k2 = np.arange(n2)[:, None].astype(np.float64)
    ang2 = 2.0 * np.pi * ((k2 * m2) % n2) / n2
    c2, s2 = np.cos(ang2), np.sin(ang2)
    f2 = np.block([[c2, s2], [-s2, c2]])
    q = np.arange(FNET_GROUP_DIM)
    angc = 2.0 * np.pi * ((q[:, None] * q[None, :]) % FNET_GROUP_DIM) / FNET_GROUP_DIM
    scale = 1.0 / math.sqrt(seq * FNET_GROUP_DIM)
    eye = np.eye(FNET_GROUPS)
    cd = np.concatenate([np.kron(eye, np.cos(angc)), np.kron(eye, np.sin(angc))], axis=0) * scale
    tr = jnp.repeat(jnp.asarray(tr, F32), FNET_WIDTH, axis=1)
    ti = jnp.repeat(jnp.asarray(ti, F32), FNET_WIDTH, axis=1)
    return (jnp.asarray(f1, BF), tr, ti, jnp.asarray(f2, BF), jnp.asarray(cd, BF))


def _fft1_kernel(x_ref, f1_ref, tr_ref, ti_ref, o_ref):
    n1 = x_ref.shape[1]
    res = jnp.dot(f1_ref[...], x_ref[0].astype(BF), preferred_element_type=F32)
    ar, ai = res[:n1], res[n1:]
    tr, ti = tr_ref[...], ti_ref[...]
    o_ref[0, 0] = ar * tr - ai * ti
    o_ref[0, 1] = ar * ti + ai * tr


def _fft2_kernel(z_ref, f2_ref, cd_ref, o_ref):
    n2 = FFT_N2
    for i in range(FFT_K1_BLOCK):
        z = jnp.concatenate([z_ref[0, 0, i], z_ref[0, 1, i]], axis=0).astype(BF)
        g = jnp.dot(f2_ref[...], z, preferred_element_type=F32)
        y = (jnp.dot(g[:n2].astype(BF), cd_ref[:FNET_WIDTH], preferred_element_type=F32)
             + jnp.dot(g[n2:].astype(BF), cd_ref[FNET_WIDTH:], preferred_element_type=F32))
        o_ref[0, :, i * FNET_WIDTH:(i + 1) * FNET_WIDTH] = y


def fourier_mix(ua, bsz, seq, tables):
    f1, tr, ti, f2, cd = tables
    n2 = FFT_N2
    n1 = seq // n2
    cols = n2 * FNET_WIDTH
    cb = FFT_COLS
    x = ua.reshape(bsz, n1, cols)
    z = pl.pallas_call(
        _fft1_kernel,
        grid=(bsz, cols // cb),
        in_specs=[pl.BlockSpec((1, n1, cb), lambda b, j: (b, 0, j)),
                  _const_spec((2 * n1, n1)),
                  pl.BlockSpec((n1, cb), lambda b, j: (0, j)),
                  pl.BlockSpec((n1, cb), lambda b, j: (0, j))],
        out_specs=pl.BlockSpec((1, 2, n1, cb), lambda b, j: (b, 0, 0, j)),
        out_shape=jax.ShapeDtypeStruct((bsz, 2, n1, cols), F32),
        compiler_params=_cparams(("parallel", "parallel")),
        name="fft1",
    )(x, f1, tr, ti)
    z = z.reshape(bsz, 2, n1, n2, FNET_WIDTH)
    kb = FFT_K1_BLOCK
    y = pl.pallas_call(
        _fft2_kernel,
        grid=(bsz, n1 // kb),
        in_specs=[pl.BlockSpec((1, 2, kb, n2, FNET_WIDTH), lambda b, j: (b, 0, j, 0, 0)),
                  _const_spec((2 * n2, 2 * n2)),
                  _const_spec((2 * FNET_WIDTH, FNET_WIDTH))],
        out_specs=pl.BlockSpec((1, n2, kb * FNET_WIDTH), lambda b, j: (b, 0, j)),
        out_shape=jax.ShapeDtypeStruct((bsz, n2, n1 * FNET_WIDTH), F32),
        compiler_params=_cparams(("parallel", "parallel")),
        name="fft2",
    )(z, f2, cd)
    return y.reshape(bsz * seq, FNET_WIDTH)


def _head_ones():
    h = np.arange(RWKV_WIDTH) // RWKV_HEAD_DIM
    return jnp.asarray((h[:, None] == h[None, :]).astype(np.float32), BF)


def _rwkv_prep_kernel(tiles_per_seq, u_ref, up_ref, un_ref, mup_ref, mun_ref, w0_ref, wup_ref, a0_ref, aup_ref,
                      gup_ref, kk_w_ref, ka_ref, rk_ref, ones_ref,
                      r_ref, kkn_ref, v_ref, krep_ref, kka_ref, logw_ref, bonus_ref, g_ref, buf):
    tm = u_ref.shape[0]
    i = pl.program_id(0)
    first = (i % tiles_per_seq) == 0
    last = (i % tiles_per_seq) == tiles_per_seq - 1
    buf[0:8] = jnp.where(first, 0.0, up_ref[...])
    buf[8:tm + 8] = u_ref[...]
    buf[tm + 8:tm + 16] = jnp.where(last, 0.0, un_ref[...])
    mup, mun = mup_ref[...], mun_ref[...]

    def shifted(lo, hi):
        u = buf[8:tm + 8, lo:hi]
        return (u + mup[:, lo:hi] * (buf[7:tm + 7, lo:hi] - u) + mun[:, lo:hi] * (buf[9:tm + 9, lo:hi] - u))

    c0 = RWKV_WIDTH
    r = shifted(0, c0)
    k = shifted(c0, 2 * c0)
    v = shifted(2 * c0, 3 * c0)
    wd = shifted(3 * c0, 3 * c0 + 128)
    ad = shifted(3 * c0 + 128, 3 * c0 + 256)
    gd = shifted(3 * c0 + 256, 3 * c0 + 384)

    ones = ones_ref[...]
    r_ref[...] = r.astype(BF)
    v_ref[...] = v.astype(BF)
    kk = k * kk_w_ref[...]
    ss = _split_dot(kk * kk, ones)
    kk = kk * lax.rsqrt(jnp.maximum(ss, 1e-24))
    kkn_ref[...] = kk.astype(BF)
    bonus_ref[...] = (_split_dot(r * k * rk_ref[...], ones) * v).astype(BF)
    g_ref[...] = _bdot(_sigmoid(gd), gup_ref[...]).astype(BF)

    wlogit = _bdot(jnp.tanh(wd), wup_ref[...])
    alogit = _bdot(ad, aup_ref[...])
    ka = ka_ref[...]
    for d in range(N_DIR):
        sl = slice(d * c0, (d + 1) * c0)
        logw_ref[d] = -DECAY_SCALE * _sigmoid(w0_ref[:, sl] + wlogit[:, sl])
        a = _sigmoid(a0_ref[:, sl] + alogit[:, sl])
        krep_ref[d] = (k * (1.0 + (a - 1.0) * ka)).astype(BF)
        kka_ref[d] = (kk * a).astype(BF)


def _lora_block(w):
    z = jnp.zeros_like(w[0])
    return jnp.concatenate([jnp.concatenate([w[0], z], axis=1), jnp.concatenate([z, w[1]], axis=1)], axis=0)


def rwkv_prep(ub, seq, p, l):
    t = ub.shape[0]
    tm = TOKEN_TILE
    nb8 = tm // 8
    row = lambda w: pl.BlockSpec((tm, w), lambda i: (i, 0))
    row2 = pl.BlockSpec((N_DIR, tm, RWKV_WIDTH), lambda i: (0, i, 0))
    vec = lambda a: a.reshape(1, -1).astype(F32)
    sds = jax.ShapeDtypeStruct
    outs = pl.pallas_call(
        functools.partial(_rwkv_prep_kernel, seq // tm),
        grid=(t // tm,),
        in_specs=[row(RWKV_IN),
                  pl.BlockSpec((8, RWKV_IN), lambda i: (jnp.maximum(i * nb8 - 1, 0), 0)),
                  pl.BlockSpec((8, RWKV_IN), lambda i: (jnp.minimum((i + 1) * nb8, t // 8 - 1), 0)),
                  _const_spec((1, RWKV_IN)), _const_spec((1, RWKV_IN)),
                  _const_spec((1, 2 * RWKV_WIDTH)), _const_spec((2 * W_LORA, 2 * RWKV_WIDTH)),
                  _const_spec((1, 2 * RWKV_WIDTH)), _const_spec((2 * A_LORA, 2 * RWKV_WIDTH)),
                  _const_spec((G_LORA, RWKV_WIDTH)),
                  _const_spec((1, RWKV_WIDTH)), _const_spec((1, RWKV_WIDTH)), _const_spec((1, RWKV_WIDTH)),
                  _const_spec((RWKV_WIDTH, RWKV_WIDTH))],
        out_specs=[row(RWKV_WIDTH), row(RWKV_WIDTH), row(RWKV_WIDTH), row2, row2, row2,
                   row(RWKV_WIDTH), row(RWKV_WIDTH)],
        out_shape=[sds((t, RWKV_WIDTH), BF)] * 3 + [sds((N_DIR, t, RWKV_WIDTH), BF)] * 2
                  + [sds((N_DIR, t, RWKV_WIDTH), F32)] + [sds((t, RWKV_WIDTH), BF)] * 2,
        scratch_shapes=[pltpu.VMEM((tm + 16, RWKV_IN), F32)],
        compiler_params=_cparams(("parallel",)),
        name="rwkv_prep",
    )(ub, ub, ub, vec(p['rwkv_mu_prev'][l]), vec(p['rwkv_mu_next'][l]),
      vec(p['rwkv_w0'][l]), _lora_block(p['rwkv_w_up'][l]).astype(BF),
      vec(p['rwkv_a0'][l]), _lora_block(p['rwkv_a_up'][l]).astype(BF),
      p['rwkv_g_up'][l].astype(BF), vec(p['rwkv_k_k'][l]), vec(p['rwkv_k_a'][l]), vec(p['rwkv_r_k'][l]),
      _head_ones())
    return outs


GROUP_HEADS = 4
GROUP = GROUP_HEADS * RWKV_HEAD_DIM
N_GROUP = RWKV_HEADS // GROUP_HEADS
SCAN_CHUNKS_PER_ITER = 4
SCAN_BATCH = 2


def _scan_kernel(r_ref, kk_ref, v_ref, krep_ref, kka_ref, logw_ref, o_ref,
                 s_ref, ar_s, t_s, mrb_s, x0_s, o0_s, bg_s, kv_s, dec_s):
    L = CHUNK
    nb = r_ref.shape[0]
    nch = r_ref.shape[1] // L
    cpi = SCAN_CHUNKS_PER_ITER
    d = pl.program_id(1)

    @pl.when(pl.program_id(2) == 0)
    def _():
        s_ref[...] = jnp.zeros_like(s_ref)

    sgn = 1 - 2 * d
    r64 = lax.broadcasted_iota(jnp.int32, (L, L), 0)
    c64 = lax.broadcasted_iota(jnp.int32, (L, L), 1)
    tri = jnp.where((r64 - c64) * sgn >= 0, 1.0, 0.0).astype(BF)
    row = lax.broadcasted_iota(jnp.int32, (L, GROUP), 0)
    sidx = lax.broadcasted_iota(jnp.int32, (L, GROUP), 1) % L
    diff = (row - sidx) * sgn
    strict = diff > 0
    incl = diff >= 0
    eye = jnp.where(row == sidx, 1.0, 0.0)
    brow = lax.broadcasted_iota(jnp.int32, (GROUP, GROUP), 0) // L
    bcol = lax.broadcasted_iota(jnp.int32, (GROUP, GROUP), 1) // L
    same_head = brow == bcol
    lane128 = lax.broadcasted_iota(jnp.int32, (L, 128), 1)
    half_ones = [jnp.where(lane128 < L, 1.0, 0.0).astype(BF), jnp.where(lane128 >= L, 1.0, 0.0).astype(BF)]
    zero_tile = jnp.zeros((L, 128), BF)

    def bdiag(x):
        xb = x.astype(BF)
        blocks = []
        for h in range(GROUP_HEADS):
            tile = xb[:, 128 * (h // 2):128 * (h // 2 + 1)] * half_ones[h % 2]
            blocks.append(jnp.concatenate([tile, zero_tile] if h < 2 else [zero_tile, tile], axis=1))
        return jnp.concatenate(blocks, axis=0)

    def off_mask(b):
        return ((row // (2 * b)) == (sidx // (2 * b))) & ((row // b) != (sidx // b))

    def phase1_body(it, carry):
        chains = []
        for q in range(cpi):
            cid = it * cpi + q
            bi = cid // nch
            j = cid % nch
            sl = pl.ds(pl.multiple_of(j * L, L), L)
            lw = logw_ref[0, bi, sl, :]
            lwh = lw.astype(BF)
            lwl = (lw - lwh.astype(F32)).astype(BF)
            g = jnp.dot(tri, lwh, preferred_element_type=F32) + jnp.dot(tri, lwl, preferred_element_type=F32)
            gtot = jnp.sum(lw, axis=0, keepdims=True)
            e_e = jnp.exp(g - lw)
            e_mg = jnp.exp(-g)
            e_gt = jnp.exp(gtot - g)
            kk = kk_ref[bi, sl, :].astype(F32)
            kka = kka_ref[0, bi, sl, :].astype(F32)
            krep = krep_ref[0, bi, sl, :].astype(F32)
            at = kk * e_e
            bh = kka * e_mg
            kh = krep * e_mg
            rh = r_ref[bi, sl, :].astype(F32) * jnp.exp(g)
            bg = kka * e_gt
            kg = krep * e_gt
            vv = v_ref[bi, sl, :]
            dec_s[bi, j] = jnp.broadcast_to(jnp.exp(gtot), (8, RWKV_WIDTH))
            for gi in range(N_GROUP):
                gs = slice(gi * GROUP, (gi + 1) * GROUP)
                ar = jnp.concatenate([at[:, gs], rh[:, gs]], axis=0).astype(BF)
                ar_s[bi, j, gi] = ar
                bg_s[bi, j, gi] = bg[:, gs].astype(BF)
                chains.append(dict(ix=(bi, j, gi), ar=ar, bh=bh[:, gs], kh=kh[:, gs], v=vv[:, gs], kg=kg[:, gs]))
        for c in chains:
            c['nb'] = _dot_t1(c['ar'], bdiag(c['bh']))
        for c in chains:
            c['nk'] = _dot_t1(c['ar'], bdiag(c['kh']))
        for c in chains:
            nk = c['nk']
            lhs = jnp.concatenate([jnp.where(strict, nk[:L], 0.0), jnp.where(incl, nk[L:], 0.0)], axis=0)
            xo = jnp.dot(lhs.astype(BF), bdiag(c['v']), preferred_element_type=F32)
            x0_s[c['ix']] = xo[:L]
            o0_s[c['ix']] = xo[L:]
        for c in chains:
            kv = _dot_t0(c['v'], c['kg'].astype(BF))
            kv_s[c['ix']] = jnp.where(same_head, kv, 0.0)
        for c in chains:
            nbm = c['nb']
            mrb_s[c['ix']] = jnp.where(incl, nbm[L:], 0.0).astype(BF)
            c['n'] = jnp.where(strict, nbm[:L], 0.0)
            c['t'] = eye - jnp.where(off_mask(1), c['n'], 0.0)
        b = 2
        while b < L:
            om = off_mask(b)
            for c in chains:
                c['p'] = jnp.dot(c['t'].astype(BF), bdiag(jnp.where(om, c['n'], 0.0)),
                                 preferred_element_type=F32)
            for c in chains:
                c['t'] = c['t'] - jnp.dot(c['p'].astype(BF), bdiag(c['t']), preferred_element_type=F32)
            b *= 2
        for c in chains:
            t_s[c['ix']] = c['t'].astype(BF)
        return carry

    lax.fori_loop(0, nb * nch // cpi, phase1_body, 0)

    def phase2_body(jj, carry):
        j = jj + d * (nch - 1 - 2 * jj)
        sl = pl.ds(pl.multiple_of(j * L, L), L)
        ids = [(bi, gi) for bi in range(nb) for gi in range(N_GROUP)]
        s = {k: s_ref[k] for k in ids}
        arh = {k: _dot_t1(ar_s[k[0], j, k[1]], s[k].astype(BF)) for k in ids}
        u = {k: jnp.dot(t_s[k[0], j, k[1]], bdiag(arh[k][:L] + x0_s[k[0], j, k[1]]), preferred_element_type=F32)
             for k in ids}
        bu = {k: _dot_t0(u[k].astype(BF), bg_s[k[0], j, k[1]]) for k in ids}
        for bi, gi in ids:
            gs = slice(gi * GROUP, (gi + 1) * GROUP)
            dec = dec_s[bi, j]
            s_ref[bi, gi] = (s[(bi, gi)] * dec[0:1, gs] + kv_s[bi, j, gi]
                             - jnp.where(same_head, bu[(bi, gi)], 0.0))
        for bi, gi in ids:
            gs = slice(gi * GROUP, (gi + 1) * GROUP)
            o_ref[0, bi, sl, gs] = (arh[(bi, gi)][L:] + o0_s[bi, j, gi]
                                    - jnp.dot(mrb_s[bi, j, gi], bdiag(u[(bi, gi)]), preferred_element_type=F32))
        return carry

    lax.fori_loop(0, nch, phase2_body, 0)


def rwkv_scan(r, kk, v, krep, kka, logw, bsz, seq):
    t = r.shape[0]
    ts = SCAN_TILE
    nb = SCAN_BATCH
    nc = seq // ts
    nch = ts // CHUNK

    def tblk(d, c):
        return c + d * (nc - 1 - 2 * c)

    shared = pl.BlockSpec((nb, ts, RWKV_WIDTH), lambda b, d, c: (b, tblk(d, c), 0))
    perdir = pl.BlockSpec((1, nb, ts, RWKV_WIDTH), lambda b, d, c: (d, b, tblk(d, c), 0))
    shape3 = (bsz, seq, RWKV_WIDTH)
    shape4 = (N_DIR, bsz, seq, RWKV_WIDTH)
    per = (nb, nch, N_GROUP)
    o = pl.pallas_call(
        _scan_kernel,
        grid=(bsz // nb, N_DIR, nc),
        in_specs=[shared, shared, shared, perdir, perdir, perdir],
        out_specs=perdir,
        out_shape=jax.ShapeDtypeStruct(shape4, F32),
        scratch_shapes=[pltpu.VMEM((nb, N_GROUP, GROUP, GROUP), F32),
                        pltpu.VMEM(per + (2 * CHUNK, GROUP), BF),
                        pltpu.VMEM(per + (CHUNK, GROUP), BF),
                        pltpu.VMEM(per + (CHUNK, GROUP), BF),
                        pltpu.VMEM(per + (CHUNK, GROUP), F32),
                        pltpu.VMEM(per + (CHUNK, GROUP), F32),
                        pltpu.VMEM(per + (CHUNK, GROUP), BF),
                        pltpu.VMEM(per + (GROUP, GROUP), F32),
                        pltpu.VMEM((nb, nch, 8, RWKV_WIDTH), F32)],
        compiler_params=_cparams(("parallel", "arbitrary", "arbitrary")),
        name="rwkv_scan",
    )(r.reshape(shape3), kk.reshape(shape3), v.reshape(shape3), krep.reshape(shape4), kka.reshape(shape4),
      logw.reshape(shape4))
    return o.reshape(N_DIR, t, RWKV_WIDTH)


CONV_HALO = 16


def _conv_kernel(tiles_per_seq, h_ref, hp_ref, hn_ref, w_ref, b_ref, lnw_ref, lnb_ref, o_ref, buf):
    tm = h_ref.shape[0]
    i = pl.program_id(0)
    first = (i % tiles_per_seq) == 0
    last = (i % tiles_per_seq) == tiles_per_seq - 1
    buf[0:CONV_HALO] = jnp.where(first, 0.0, hp_ref[...])
    buf[CONV_HALO:tm + CONV_HALO] = h_ref[...]
    buf[tm + CONV_HALO:tm + 2 * CONV_HALO] = jnp.where(last, 0.0, hn_ref[...])
    acc = jnp.zeros((tm, CONV_WIDTH), F32) + b_ref[...]
    for k in range(CONV_KERNEL):
        off = CONV_HALO - CONV_KERNEL // 2 + k
        acc = acc + w_ref[k:k + 1, :] * buf[off:off + tm, :]
    mu = jnp.mean(acc, axis=-1, keepdims=True)
    xc = acc - mu
    var = jnp.mean(xc * xc, axis=-1, keepdims=True)
    y = xc * lax.rsqrt(var + LN_EPS) * lnw_ref[...] + lnb_ref[...]
    o_ref[...] = y * _sigmoid(y)


def conformer_conv(hc, seq, p, l):
    t = hc.shape[0]
    tm = TOKEN_TILE
    nbh = tm // CONV_HALO
    row = pl.BlockSpec((tm, CONV_WIDTH), lambda i: (i, 0))
    vec = lambda a: a.reshape(1, -1).astype(F32)
    return pl.pallas_call(
        functools.partial(_conv_kernel, seq // tm),
        grid=(t // tm,),
        in_specs=[row,
                  pl.BlockSpec((CONV_HALO, CONV_WIDTH), lambda i: (jnp.maximum(i * nbh - 1, 0), 0)),
                  pl.BlockSpec((CONV_HALO, CONV_WIDTH),
                               lambda i: (jnp.minimum((i + 1) * nbh, t // CONV_HALO - 1), 0)),
                  _const_spec((CONV_KERNEL, CONV_WIDTH)), _const_spec((1, CONV_WIDTH)),
                  _const_spec((1, CONV_WIDTH)), _const_spec((1, CONV_WIDTH))],
        out_specs=row,
        out_shape=jax.ShapeDtypeStruct((t, CONV_WIDTH), F32),
        scratch_shapes=[pltpu.VMEM((tm + 2 * CONV_HALO, CONV_WIDTH), F32)],
        compiler_params=_cparams(("parallel",)),
        name="conformer_conv",
    )(hc, hc, hc, p['conv_dw_w'][l], vec(p['conv_dw_b'][l]), vec(p['conv_ln_w'][l]), vec(p['conv_ln_b'][l]))


def _merge_kernel(x_ref, ya_ref, o_ref, bonus_ref, g_ref, yc_ref, gate_ref, ones_ref, gnw_ref, gnb_ref,
                  wa_ref, wb_ref, wc_ref, wo_ref, out_ref):
    ones = ones_ref[...]
    o = o_ref[0] + o_ref[1]
    inv = 1.0 / RWKV_HEAD_DIM
    mu = _split_dot(o, ones) * inv
    oc = o - mu
    var = _split_dot(oc * oc, ones) * inv
    on = oc * lax.rsqrt(var + GN_EPS) * gnw_ref[...] + gnb_ref[...]
    yb = (on + bonus_ref[...]) * g_ref[...]
    m = gate_ref[:, 0:D_MODEL].astype(F32) * _bdot(ya_ref[...], wa_ref[...])
    m = m + gate_ref[:, D_MODEL:2 * D_MODEL].astype(F32) * _bdot(yb, wb_ref[...])
    m = m + gate_ref[:, 2 * D_MODEL:3 * D_MODEL].astype(F32) * _bdot(yc_ref[...], wc_ref[...])
    out_ref[...] = x_ref[...] + _bdot(m, wo_ref[...])


def merge(x, ya, o, bonus, g, yc, gates, p, l):
    t = x.shape[0]
    tm = TOKEN_TILE
    row = lambda w: pl.BlockSpec((tm, w), lambda i: (i, 0))
    vec = lambda a: a.reshape(1, -1).astype(F32)
    return pl.pallas_call(
        _merge_kernel,
        grid=(t // tm,),
        in_specs=[row(D_MODEL), row(FNET_WIDTH),
                  pl.BlockSpec((N_DIR, tm, RWKV_WIDTH), lambda i: (0, i, 0)),
                  row(RWKV_WIDTH), row(RWKV_WIDTH), row(CONV_WIDTH), row(3 * D_MODEL),
                  _const_spec((RWKV_WIDTH, RWKV_WIDTH)), _const_spec((1, RWKV_WIDTH)), _const_spec((1, RWKV_WIDTH)),
                  _const_spec((FNET_WIDTH, D_MODEL)), _const_spec((RWKV_WIDTH, D_MODEL)),
                  _const_spec((CONV_WIDTH, D_MODEL)), _const_spec((D_MODEL, D_MODEL))],
        out_specs=row(D_MODEL),
        out_shape=jax.ShapeDtypeStruct((t, D_MODEL), F32),
        compiler_params=_cparams(("parallel",)),
        name="merge",
    )(x, ya, o, bonus, g, yc, gates, _head_ones(), vec(p['rwkv_gn_w'][l]), vec(p['rwkv_gn_b'][l]),
      p['fnet_w'][l].astype(BF), p['rwkv_w_o'][l].astype(BF), p['conv_w_o'][l].astype(BF),
      p['mix_w_out'][l].astype(BF))


FFN_HALO = 16


def _erf(x):
    return lax.erf(x)


def _ffn_kernel(tiles_per_seq, final, x_ref, xp_ref, xn_ref, g_ref, wup_ref, dww_ref, dwb_ref, wdn_ref, gf_ref,
                out_ref, xn_s, h_s):
    tm = x_ref.shape[0]
    i = pl.program_id(0)
    first = (i % tiles_per_seq) == 0
    last = (i % tiles_per_seq) == tiles_per_seq - 1
    g = g_ref[...]
    hl = FFN_HALO
    xn_s[0:hl] = _rms(xp_ref[...], g).astype(BF)
    xn_s[hl:tm + hl] = _rms(x_ref[...], g).astype(BF)
    xn_s[tm + hl:tm + 2 * hl] = _rms(xn_ref[...], g).astype(BF)
    rows = lax.broadcasted_iota(jnp.int32, (tm + 2 * hl, 1), 0)
    pad = (first & (rows < hl)) | (last & (rows >= tm + hl))
    acc = jnp.zeros((tm, D_MODEL), F32)
    for lo, hi in FF_BLOCKS:
        h = jnp.dot(xn_s[...], wup_ref[:, lo:hi], preferred_element_type=F32)
        h_s[:, 0:hi - lo] = jnp.where(pad, 0.0, h)
        hc = (dww_ref[0:1, lo:hi] * h_s[hl - 1:tm + hl - 1, 0:hi - lo]
              + dww_ref[1:2, lo:hi] * h_s[hl:tm + hl, 0:hi - lo]
              + dww_ref[2:3, lo:hi] * h_s[hl + 1:tm + hl + 1, 0:hi - lo] + dwb_ref[:, lo:hi])
        gate = jnp.dot(xn_s[hl:tm + hl], wup_ref[:, D_FF + lo:D_FF + hi], preferred_element_type=F32)
        act = 0.5 * hc * (1.0 + _erf(hc * (1.0 / math.sqrt(2.0)))) * gate
        acc = acc + jnp.dot(act.astype(BF), wdn_ref[lo:hi, :], preferred_element_type=F32)
    y = x_ref[...] + acc
    if final:
        y = _rms(y, gf_ref[...])
    out_ref[...] = y


def ffn(x, seq, p, l, final):
    t = x.shape[0]
    tm = TOKEN_TILE
    hl = FFN_HALO
    nbh = tm // hl
    row = pl.BlockSpec((tm, D_MODEL), lambda i: (i, 0))
    vec = lambda a: a.reshape(1, -1).astype(F32)
    fb = max(hi - lo for lo, hi in FF_BLOCKS)
    return pl.pallas_call(
        functools.partial(_ffn_kernel, seq // tm, final),
        grid=(t // tm,),
        in_specs=[row,
                  pl.BlockSpec((hl, D_MODEL), lambda i: (jnp.maximum(i * nbh - 1, 0), 0)),
                  pl.BlockSpec((hl, D_MODEL), lambda i: (jnp.minimum((i + 1) * nbh, t // hl - 1), 0)),
                  _const_spec((1, D_MODEL)), _const_spec((D_MODEL, 2 * D_FF)),
                  _const_spec((3, D_FF)), _const_spec((1, D_FF)), _const_spec((D_FF, D_MODEL)),
                  _const_spec((1, D_MODEL))],
        out_specs=row,
        out_shape=jax.ShapeDtypeStruct((t, D_MODEL), F32),
        scratch_shapes=[pltpu.VMEM((tm + 2 * hl, D_MODEL), BF), pltpu.VMEM((tm + 2 * hl, fb), F32)],
        compiler_params=_cparams(("parallel",)),
        name="ffn",
    )(x, x, x, vec(p['ffn_norm_g'][l]), p['ffn_w_up'][l].astype(BF), p['ffn_dw_w'][l],
      vec(p['ffn_dw_b'][l]), p['ffn_w_down'][l].astype(BF), vec(p['final_norm_g']))


def encoder(x3, p):
    bsz, seq, _ = x3.shape
    depth = p['w_in'].shape[0]
    x = x3.reshape(bsz * seq, D_MODEL)
    tables = _fft_tables(seq)
    for l in range(depth):
        ua, ub, hc, gates = mixer_in(x, p['attn_norm_g'][l], p['w_in'][l].astype(BF))
        ya = fourier_mix(ua, bsz, seq, tables)
        r, kk, v, krep, kka, logw, bonus, g = rwkv_prep(ub, seq, p, l)
        o = rwkv_scan(r, kk, v, krep, kka, logw, bsz, seq)
        yc = conformer_conv(hc, seq, p, l)
        x = merge(x, ya, o, bonus, g, yc, gates, p, l)
        x = ffn(x, seq, p, l, final=(l == depth - 1))
    return x.reshape(bsz, seq, D_MODEL)


def kernel(x_prompt, x_sample, attn_norm_g, w_in, fnet_w, rwkv_mu_prev, rwkv_mu_next, rwkv_w0, rwkv_w_up,
           rwkv_a0, rwkv_a_up, rwkv_g_up, rwkv_k_k, rwkv_k_a, rwkv_r_k, rwkv_gn_w, rwkv_gn_b, rwkv_w_o,
           conv_dw_w, conv_dw_b, conv_ln_w, conv_ln_b, conv_w_o, mix_w_out, ffn_norm_g, ffn_w_up,
           ffn_dw_w, ffn_dw_b, ffn_w_down, final_norm_g):
    p = dict(attn_norm_g=attn_norm_g, w_in=w_in, fnet_w=fnet_w, rwkv_mu_prev=rwkv_mu_prev,
             rwkv_mu_next=rwkv_mu_next, rwkv_w0=rwkv_w0, rwkv_w_up=rwkv_w_up, rwkv_a0=rwkv_a0,
             rwkv_a_up=rwkv_a_up, rwkv_g_up=rwkv_g_up, rwkv_k_k=rwkv_k_k, rwkv_k_a=rwkv_k_a, rwkv_r_k=rwkv_r_k,
             rwkv_gn_w=rwkv_gn_w, rwkv_gn_b=rwkv_gn_b, rwkv_w_o=rwkv_w_o, conv_dw_w=conv_dw_w,
             conv_dw_b=conv_dw_b, conv_ln_w=conv_ln_w, conv_ln_b=conv_ln_b, conv_w_o=conv_w_o,
             mix_w_out=mix_w_out, ffn_norm_g=ffn_norm_g, ffn_w_up=ffn_w_up, ffn_dw_w=ffn_dw_w,
             ffn_dw_b=ffn_dw_b, ffn_w_down=ffn_w_down, final_norm_g=final_norm_g)
    return (encoder(x_prompt, p), encoder(x_sample, p))
```

```python
import functools
import math

import numpy as np
import jax
import jax.numpy as jnp
from jax import lax
from jax.experimental import pallas as pl
from jax.experimental.pallas import tpu as pltpu

D_MODEL = 1024
FNET_GROUPS = 4
FNET_GROUP_DIM = 64
FNET_WIDTH = 256
RWKV_HEADS = 8
RWKV_HEAD_DIM = 64
RWKV_WIDTH = 512
N_DIR = 2
W_LORA = 64
A_LORA = 64
G_LORA = 128
RWKV_IN = 1920
CONV_WIDTH = 256
CONV_KERNEL = 31
IN_COLS = 5760
D_FF = 2816
RMS_EPS = 1e-6
LN_EPS = 1e-5
GN_EPS = 64e-5
DECAY_SCALE = math.exp(-0.5)

COL_A = 0
COL_B = FNET_WIDTH
COL_C = COL_B + RWKV_IN
COL_G = COL_C + 2 * CONV_WIDTH

BF = jnp.bfloat16
F32 = jnp.float32

TOKEN_TILE = 512
CHUNK = 64
FFT_N2 = 64
FFT_COLS = 2048
FFT_K1_BLOCK = 8
VMEM_LIMIT = 56 * 1024 * 1024
FF_BLOCKS = ((0, 1024), (1024, 2048), (2048, 2816))


def _cparams(sem):
    return pltpu.CompilerParams(dimension_semantics=sem, vmem_limit_bytes=VMEM_LIMIT)


def _bdot(a, b):
    return jnp.dot(a.astype(BF), b.astype(BF), preferred_element_type=F32)


def _split_dot(a, b_exact):
    ah = a.astype(BF)
    al = (a - ah.astype(F32)).astype(BF)
    return (jnp.dot(ah, b_exact, preferred_element_type=F32)
            + jnp.dot(al, b_exact, preferred_element_type=F32))


def _dot_t0(a, b):
    return lax.dot_general(a, b, (((0,), (0,)), ((), ())), preferred_element_type=F32)


def _dot_t1(a, b):
    return lax.dot_general(a, b, (((1,), (1,)), ((), ())), preferred_element_type=F32)


def _sigmoid(x):
    return 1.0 / (1.0 + jnp.exp(-x))


def _rms(x, g):
    return x * lax.rsqrt(jnp.mean(x * x, axis=-1, keepdims=True) + RMS_EPS) * g


def _const_spec(shape):
    nd = len(shape)
    return pl.BlockSpec(shape, lambda *_: (0,) * nd)


def _mixer_in_kernel(x_ref, g_ref, w_ref, ua_ref, ub_ref, hc_ref, gate_ref):
    xn = _rms(x_ref[...], g_ref[...]).astype(BF)
    ua_ref[...] = jnp.dot(xn, w_ref[:, COL_A:COL_B], preferred_element_type=F32).astype(BF)
    for lo, hi in ((0, 1024), (1024, RWKV_IN)):
        ub_ref[:, lo:hi] = jnp.dot(xn, w_ref[:, COL_B + lo:COL_B + hi], preferred_element_type=F32).astype(BF)
    uc = jnp.dot(xn, w_ref[:, COL_C:COL_G], preferred_element_type=F32)
    hc_ref[...] = uc[:, :CONV_WIDTH] * _sigmoid(uc[:, CONV_WIDTH:])
    for j in range(3):
        lo = j * D_MODEL
        ug = jnp.dot(xn, w_ref[:, COL_G + lo:COL_G + lo + D_MODEL], preferred_element_type=F32)
        gate_ref[:, lo:lo + D_MODEL] = _sigmoid(ug).astype(BF)


def mixer_in(x, g, w_in_bf):
    t = x.shape[0]
    tm = TOKEN_TILE
    row = lambda w: pl.BlockSpec((tm, w), lambda i: (i, 0))
    return pl.pallas_call(
        _mixer_in_kernel,
        grid=(t // tm,),
        in_specs=[row(D_MODEL), _const_spec((1, D_MODEL)), _const_spec((D_MODEL, IN_COLS))],
        out_specs=[row(FNET_WIDTH), row(RWKV_IN), row(CONV_WIDTH), row(3 * D_MODEL)],
        out_shape=[jax.ShapeDtypeStruct((t, FNET_WIDTH), BF), jax.ShapeDtypeStruct((t, RWKV_IN), BF),
                   jax.ShapeDtypeStruct((t, CONV_WIDTH), F32), jax.ShapeDtypeStruct((t, 3 * D_MODEL), BF)],
        compiler_params=_cparams(("parallel",)),
        name="mixer_in",
    )(x, g.reshape(1, D_MODEL), w_in_bf)


def _fft_tables(seq):
    n2 = FFT_N2
    n1 = seq // n2
    k1 = np.arange(n1)[:, None].astype(np.float64)
    m1 = np.arange(n1)[None, :].astype(np.float64)
    ang1 = 2.0 * np.pi * ((k1 * m1) % n1) / n1
    f1 = np.concatenate([np.cos(ang1), -np.sin(ang1)], axis=0)
    m2 = np.arange(n2)[None, :].astype(np.float64)
    angt = 2.0 * np.pi * ((k1 * m2) % seq) / seq
    tr, ti = np.cos(angt), -np.sin(angt)
    k2 = np.arange(n2)[:, None].astype(np.float64)
    ang2 = 2.0 * np.pi * ((k2 * m2) % n2) / n2
    c2, s2 = np.cos(ang2), np.sin(ang2)
    f2 = np.block([[c2, s2], [-s2, c2]])
    q = np.arange(FNET_GROUP_DIM)
    angc = 2.0 * np.pi * ((q[:, None] * q[None, :]) % FNET_GROUP_DIM) / FNET_GROUP_DIM
    scale = 1.0 / math.sqrt(seq * FNET_GROUP_DIM)
    eye = np.eye(FNET_GROUPS)
    cd = np.concatenate([np.kron(eye, np.cos(angc)), np.kron(eye, np.sin(angc))], axis=0) * scale
    tr = jnp.repeat(jnp.asarray(tr, F32), FNET_WIDTH, axis=1)
    ti = jnp.repeat(jnp.asarray(ti, F32), FNET_WIDTH, axis=1)
    return (jnp.asarray(f1, BF), tr, ti, jnp.asarray(f2, BF), jnp.asarray(cd, BF))


def _fft1_kernel(x_ref, f1_ref, tr_ref, ti_ref, o_ref):
    n1 = x_ref.shape[1]
    res = jnp.dot(f1_ref[...], x_ref[0].astype(BF), preferred_element_type=F32)
    ar, ai = res[:n1], res[n1:]
    tr, ti = tr_ref[...], ti_ref[...]
    o_ref[0, 0] = ar * tr - ai * ti
    o_ref[0, 1] = ar * ti + ai * tr


def _fft2_kernel(z_ref, f2_ref, cd_ref, o_ref):
    n2 = FFT_N2
    for i in range(FFT_K1_BLOCK):
        z = jnp.concatenate([z_ref[0, 0, i], z_ref[0, 1, i]], axis=0).astype(BF)
        g = jnp.dot(f2_ref[...], z, preferred_element_type=F32)
        y = (jnp.dot(g[:n2].astype(BF), cd_ref[:FNET_WIDTH], preferred_element_type=F32)
             + jnp.dot(g[n2:].astype(BF), cd_ref[FNET_WIDTH:], preferred_element_type=F32))
        o_ref[0, :, i * FNET_WIDTH:(i + 1) * FNET_WIDTH] = y


def fourier_mix(ua, bsz, seq, tables):
    f1, tr, ti, f2, cd = tables
    n2 = FFT_N2
    n1 = seq // n2
    cols = n2 * FNET_WIDTH
    cb = FFT_COLS
    x = ua.reshape(bsz, n1, cols)
    z = pl.pallas_call(
        _fft1_kernel,
        grid=(bsz, cols // cb),
        in_specs=[pl.BlockSpec((1, n1, cb), lambda b, j: (b, 0, j)),
                  _const_spec((2 * n1, n1)),
                  pl.BlockSpec((n1, cb), lambda b, j: (0, j)),
                  pl.BlockSpec((n1, cb), lambda b, j: (0, j))],
        out_specs=pl.BlockSpec((1, 2, n1, cb), lambda b, j: (b, 0, 0, j)),
        out_shape=jax.ShapeDtypeStruct((bsz, 2, n1, cols), F32),
        compiler_params=_cparams(("parallel", "parallel")),
        name="fft1",
    )(x, f1, tr, ti)
    z = z.reshape(bsz, 2, n1, n2, FNET_WIDTH)
    kb = FFT_K1_BLOCK
    y = pl.pallas_call(
        _fft2_kernel,
        grid=(bsz, n1 // kb),
        in_specs=[pl.BlockSpec((1, 2, kb, n2, FNET_WIDTH), lambda b, j: (b, 0, j, 0, 0)),
                  _const_spec((2 * n2, 2 * n2)),
                  _const_spec((2 * FNET_WIDTH, FNET_WIDTH))],
        out_specs=pl.BlockSpec((1, n2, kb * FNET_WIDTH), lambda b, j: (b, 0, j)),
        out_shape=jax.ShapeDtypeStruct((bsz, n2, n1 * FNET_WIDTH), F32),
        compiler_params=_cparams(("parallel", "parallel")),
        name="fft2",
    )(z, f2, cd)
    return y.reshape(bsz * seq, FNET_WIDTH)


PREP_HALO = 16


def _head_ones():
    h = np.arange(RWKV_WIDTH) // RWKV_HEAD_DIM
    return jnp.asarray((h[:, None] == h[None, :]).astype(np.float32), BF)


def _rwkv_prep_kernel(tiles_per_seq, u_ref, up_ref, un_ref, mup_ref, mun_ref, w0_ref, wup_ref, a0_ref, aup_ref,
                      gup_ref, kk_w_ref, ka_ref, rk_ref, ones_ref,
                      r_ref, kkn_ref, v_ref, krep_ref, kka_ref, logw_ref, bonus_ref, g_ref, buf):
    tm = u_ref.shape[0]
    i = pl.program_id(0)
    first = (i % tiles_per_seq) == 0
    last = (i % tiles_per_seq) == tiles_per_seq - 1
    buf[0:8] = jnp.where(first, 0.0, up_ref[...].astype(F32)[PREP_HALO - 8:])
    buf[8:tm + 8] = u_ref[...].astype(F32)
    buf[tm + 8:tm + 16] = jnp.where(last, 0.0, un_ref[...].astype(F32)[:8])
    mup, mun = mup_ref[...], mun_ref[...]

    def shifted(lo, hi):
        u = buf[8:tm + 8, lo:hi]
        return (u + mup[:, lo:hi] * (buf[7:tm + 7, lo:hi] - u) + mun[:, lo:hi] * (buf[9:tm + 9, lo:hi] - u))

    c0 = RWKV_WIDTH
    r = shifted(0, c0)
    k = shifted(c0, 2 * c0)
    v = shifted(2 * c0, 3 * c0)
    wd = shifted(3 * c0, 3 * c0 + 128)
    ad = shifted(3 * c0 + 128, 3 * c0 + 256)
    gd = shifted(3 * c0 + 256, 3 * c0 + 384)

    ones = ones_ref[...]
    r_ref[...] = r.astype(BF)
    v_ref[...] = v.astype(BF)
    kk = k * kk_w_ref[...]
    ss = _split_dot(kk * kk, ones)
    kk = kk * lax.rsqrt(jnp.maximum(ss, 1e-24))
    kkn_ref[...] = kk.astype(BF)
    bonus_ref[...] = (_split_dot(r * k * rk_ref[...], ones) * v).astype(BF)
    g_ref[...] = _bdot(_sigmoid(gd), gup_ref[...]).astype(BF)

    wlogit = _bdot(jnp.tanh(wd), wup_ref[...])
    alogit = _bdot(ad, aup_ref[...])
    ka = ka_ref[...]
    for d in range(N_DIR):
        sl = slice(d * c0, (d + 1) * c0)
        logw_ref[d] = -DECAY_SCALE * _sigmoid(w0_ref[:, sl] + wlogit[:, sl])
        a = _sigmoid(a0_ref[:, sl] + alogit[:, sl])
        krep_ref[d] = (k * (1.0 + (a - 1.0) * ka)).astype(BF)
        kka_ref[d] = (kk * a).astype(BF)


def _lora_block(w):
    z = jnp.zeros_like(w[0])
    return jnp.concatenate([jnp.concatenate([w[0], z], axis=1), jnp.concatenate([z, w[1]], axis=1)], axis=0)


def rwkv_prep(ub, seq, p, l):
    t = ub.shape[0]
    tm = TOKEN_TILE
    nbh = tm // PREP_HALO
    row = lambda w: pl.BlockSpec((tm, w), lambda i: (i, 0))
    row2 =pl.BlockSpec((N_DIR, tm, RWKV_WIDTH), lambda i: (0, i, 0))
    vec = lambda a: a.reshape(1, -1).astype(F32)
    sds = jax.ShapeDtypeStruct
    outs = pl.pallas_call(
        functools.partial(_rwkv_prep_kernel, seq // tm),
        grid=(t // tm,),
        in_specs=[row(RWKV_IN),
                  pl.BlockSpec((PREP_HALO, RWKV_IN), lambda i: (jnp.maximum(i * nbh - 1, 0), 0)),
                  pl.BlockSpec((PREP_HALO, RWKV_IN), lambda i: (jnp.minimum((i + 1) * nbh, t // PREP_HALO - 1), 0)),
                  _const_spec((1, RWKV_IN)), _const_spec((1, RWKV_IN)),
                  _const_spec((1, 2 * RWKV_WIDTH)), _const_spec((2 * W_LORA, 2 * RWKV_WIDTH)),
                  _const_spec((1, 2 * RWKV_WIDTH)), _const_spec((2 * A_LORA, 2 * RWKV_WIDTH)),
                  _const_spec((G_LORA, RWKV_WIDTH)),
                  _const_spec((1, RWKV_WIDTH)), _const_spec((1, RWKV_WIDTH)), _const_spec((1, RWKV_WIDTH)),
                  _const_spec((RWKV_WIDTH, RWKV_WIDTH))],
        out_specs=[row(RWKV_WIDTH), row(RWKV_WIDTH), row(RWKV_WIDTH), row2, row2, row2,
                   row(RWKV_WIDTH), row(RWKV_WIDTH)],
        out_shape=[sds((t, RWKV_WIDTH), BF)] * 3 + [sds((N_DIR, t, RWKV_WIDTH), BF)] * 2
                  + [sds((N_DIR, t, RWKV_WIDTH), F32)] + [sds((t, RWKV_WIDTH), BF)] * 2,
        scratch_shapes=[pltpu.VMEM((tm + 16, RWKV_IN), F32)],
        compiler_params=_cparams(("parallel",)),
        name="rwkv_prep",
    )(ub, ub, ub, vec(p['rwkv_mu_prev'][l]), vec(p['rwkv_mu_next'][l]),
      vec(p['rwkv_w0'][l]), _lora_block(p['rwkv_w_up'][l]).astype(BF),
      vec(p['rwkv_a0'][l]), _lora_block(p['rwkv_a_up'][l]).astype(BF),
      p['rwkv_g_up'][l].astype(BF), vec(p['rwkv_k_k'][l]), vec(p['rwkv_k_a'][l]), vec(p['rwkv_r_k'][l]),
      _head_ones())
    return outs


GROUP_HEADS = 4
GROUP = GROUP_HEADS * RWKV_HEAD_DIM
N_GROUP = RWKV_HEADS // GROUP_HEADS
SCAN_CHUNKS_PER_ITER = 4
SCAN_BATCH = 8
SCAN_TOKENS = 1024


def _scan_kernel(r_ref, kk_ref, v_ref, krep_ref, kka_ref, logw_ref, o_ref,
                 s_ref, ar_s, t_s, mrb_s, x0_s, o0_s, bg_s, kv_s, dec_s):
    L = CHUNK
    nb = r_ref.shape[0]
    nch = r_ref.shape[1] // L
    cpi = SCAN_CHUNKS_PER_ITER
    d = pl.program_id(1)

    @pl.when(pl.program_id(2) == 0)
    def _():
        s_ref[...] = jnp.zeros_like(s_ref)

    sgn = 1 - 2 * d
    r64 = lax.broadcasted_iota(jnp.int32, (L, L), 0)
    c64 = lax.broadcasted_iota(jnp.int32, (L, L), 1)
    tri = jnp.where((r64 - c64) * sgn >= 0, 1.0, 0.0).astype(BF)
    row = lax.broadcasted_iota(jnp.int32, (L, GROUP), 0)
    sidx = lax.broadcasted_iota(jnp.int32, (L, GROUP), 1) % L
    diff = (row - sidx) * sgn
    strict = diff > 0
    incl = diff >= 0
    eye = jnp.where(row == sidx, 1.0, 0.0)
    brow = lax.broadcasted_iota(jnp.int32, (GROUP, GROUP), 0) // L
    bcol = lax.broadcasted_iota(jnp.int32, (GROUP, GROUP), 1) // L
    same_head = brow == bcol
    lane128 = lax.broadcasted_iota(jnp.int32, (L, 128), 1)
    half_ones = [jnp.where(lane128 < L, 1.0, 0.0).astype(BF), jnp.where(lane128 >= L, 1.0, 0.0).astype(BF)]
    zero_tile = jnp.zeros((L, 128), BF)

    def bdiag(x):
        xb = x.astype(BF)
        blocks = []
        for h in range(GROUP_HEADS):
            tile = xb[:, 128 * (h // 2):128 * (h // 2 + 1)] * half_ones[h % 2]
            blocks.append(jnp.concatenate([tile, zero_tile] if h < 2 else [zero_tile, tile], axis=1))
        return jnp.concatenate(blocks, axis=0)

    def off_mask(b):
        return ((row // (2 * b)) == (sidx // (2 * b))) & ((row // b) != (sidx // b))

    def phase1_body(it, carry):
        chains = []
        for q in range(cpi):
            cid = it * cpi + q
            bi = cid // nch
            j = cid % nch
            sl = pl.ds(pl.multiple_of(j * L, L), L)
            lw = logw_ref[0, bi, sl, :]
            lwh = lw.astype(BF)
            lwl = (lw - lwh.astype(F32)).astype(BF)
            g = jnp.dot(tri, lwh, preferred_element_type=F32) + jnp.dot(tri, lwl, preferred_element_type=F32)
            gtot = jnp.sum(lw, axis=0, keepdims=True)
            e_e = jnp.exp(g - lw)
            e_mg = jnp.exp(-g)
            e_gt = jnp.exp(gtot - g)
            kk = kk_ref[bi, sl, :].astype(F32)
            kka = kka_ref[0, bi, sl, :].astype(F32)
            krep = krep_ref[0, bi, sl, :].astype(F32)
            at = kk * e_e
            bh = kka * e_mg
            kh = krep * e_mg
            rh = r_ref[bi, sl, :].astype(F32) * jnp.exp(g)
            bg = kka * e_gt
            kg = krep * e_gt
            vv = v_ref[bi, sl, :]
            dec_s[bi, j] = jnp.broadcast_to(jnp.exp(gtot), (8, RWKV_WIDTH))
            for gi in range(N_GROUP):
                gs = slice(gi * GROUP, (gi + 1) * GROUP)
                ar = jnp.concatenate([at[:, gs], rh[:, gs]], axis=0).astype(BF)
                ar_s[bi, j, gi] = ar
                bg_s[bi, j, gi] = bg[:, gs].astype(BF)
                chains.append(dict(ix=(bi, j, gi), ar=ar, bh=bh[:, gs], kh=kh[:, gs], v=vv[:, gs], kg=kg[:, gs]))
        for c in chains:
            c['nb'] = _dot_t1(c['ar'], bdiag(c['bh']))
        for c in chains:
            c['nk'] = _dot_t1(c['ar'], bdiag(c['kh']))
        for c in chains:
            nk = c['nk']
            lhs = jnp.concatenate([jnp.where(strict, nk[:L], 0.0), jnp.where(incl, nk[L:], 0.0)], axis=0)
            xo = jnp.dot(lhs.astype(BF), bdiag(c['v']), preferred_element_type=F32)
            x0_s[c['ix']] = xo[:L]
            o0_s[c['ix']] = xo[L:]
        for c in chains:
            kv = _dot_t0(c['v'], c['kg'].astype(BF))
            kv_s[c['ix']] = jnp.where(same_head, kv, 0.0)
        for c in chains:
            nbm = c['nb']
            mrb_s[c['ix']] = jnp.where(incl, nbm[L:], 0.0).astype(BF)
            c['n'] = jnp.where(strict, nbm[:L], 0.0)
            c['t'] = eye - jnp.where(off_mask(1), c['n'], 0.0)
        b = 2
        while b < L:
            om = off_mask(b)
            for c in chains:
                c['p'] = jnp.dot(c['t'].astype(BF), bdiag(jnp.where(om, c['n'], 0.0)),
                                 preferred_element_type=F32)
            for c in chains:
                c['t'] = c['t'] - jnp.dot(c['p'].astype(BF), bdiag(c['t']), preferred_element_type=F32)
            b *= 2
        for c in chains:
            t_s[c['ix']] = c['t'].astype(BF)
        return carry

    lax.fori_loop(0, nb * nch // cpi, phase1_body, 0)

    def phase2_body(jj, carry):
        j = jj + d * (nch - 1 - 2 * jj)
        sl = pl.ds(pl.multiple_of(j * L, L), L)
        ids = [(bi, gi) for bi in range(nb) for gi in range(N_GROUP)]
        s = {k: s_ref[k] for k in ids}
        arh = {k: _dot_t1(ar_s[k[0], j, k[1]], s[k].astype(BF)) for k in ids}
        u = {k: jnp.dot(t_s[k[0], j, k[1]], bdiag(arh[k][:L] + x0_s[k[0], j, k[1]]), preferred_element_type=F32)
             for k in ids}
        bu = {k: _dot_t0(u[k].astype(BF), bg_s[k[0], j, k[1]]) for k in ids}
        for bi, gi in ids:
            gs = slice(gi * GROUP, (gi + 1) * GROUP)
            dec = dec_s[bi, j]
            s_ref[bi, gi] = (s[(bi, gi)] * dec[0:1, gs] + kv_s[bi, j, gi]
                             - jnp.where(same_head, bu[(bi, gi)], 0.0))
        for bi, gi in ids:
            gs = slice(gi * GROUP, (gi + 1) * GROUP)
            o = (arh[(bi, gi)][L:] + o0_s[bi, j, gi]
                 - jnp.dot(mrb_s[bi, j, gi], bdiag(u[(bi, gi)]), preferred_element_type=F32))
            o_ref[0, bi, sl, gs] = o.astype(BF)
        return carry

    lax.fori_loop(0, nch, phase2_body, 0)


def rwkv_scan(r, kk, v, krep, kka, logw, bsz, seq):
    t = r.shape[0]
    nb = min(SCAN_BATCH, bsz)
    ts = SCAN_TOKENS // nb
    nc = seq // ts
    nch = ts // CHUNK

    def tblk(d, c):
        return c + d * (nc - 1 - 2 * c)

    shared = pl.BlockSpec((nb, ts, RWKV_WIDTH), lambda b, d, c: (b, tblk(d, c), 0))
    perdir = pl.BlockSpec((1, nb, ts, RWKV_WIDTH), lambda b, d, c: (d, b, tblk(d, c), 0))
    shape3 = (bsz, seq, RWKV_WIDTH)
    shape4 = (N_DIR, bsz, seq, RWKV_WIDTH)
    per = (nb, nch, N_GROUP)
    o = pl.pallas_call(
        _scan_kernel,
        grid=(bsz // nb, N_DIR, nc),
        in_specs=[shared, shared, shared, perdir, perdir, perdir],
        out_specs=perdir,
        out_shape=jax.ShapeDtypeStruct(shape4, BF),
        scratch_shapes=[pltpu.VMEM((nb, N_GROUP, GROUP, GROUP), F32),
                        pltpu.VMEM(per + (2 * CHUNK, GROUP), BF),
                        pltpu.VMEM(per + (CHUNK, GROUP), BF),
                        pltpu.VMEM(per + (CHUNK, GROUP), BF),
                        pltpu.VMEM(per + (CHUNK, GROUP), F32),
                        pltpu.VMEM(per + (CHUNK, GROUP), F32),
                        pltpu.VMEM(per + (CHUNK, GROUP), BF),
                        pltpu.VMEM(per + (GROUP, GROUP), F32),
                        pltpu.VMEM((nb, nch, 8, RWKV_WIDTH), F32)],
        compiler_params=_cparams(("parallel", "arbitrary", "arbitrary")),
        name="rwkv_scan",
    )(r.reshape(shape3), kk.reshape(shape3), v.reshape(shape3), krep.reshape(shape4), kka.reshape(shape4),
      logw.reshape(shape4))
    return o.reshape(N_DIR, t, RWKV_WIDTH)


CONV_HALO = 16


def _conv_kernel(tiles_per_seq, h_ref, hp_ref, hn_ref, w_ref, b_ref, lnw_ref, lnb_ref, o_ref, buf, part):
    tm = h_ref.shape[0]
    i = pl.program_id(0)
    first = (i % tiles_per_seq) == 0
    last = (i % tiles_per_seq) == tiles_per_seq - 1
    buf[0:CONV_HALO] = jnp.where(first, 0.0, hp_ref[...])
    buf[CONV_HALO:tm + CONV_HALO] = h_ref[...]
    buf[tm + CONV_HALO:tm + 2 * CONV_HALO] = jnp.where(last, 0.0, hn_ref[...])
    base = CONV_HALO - CONV_KERNEL // 2
    acc = jnp.zeros((tm, CONV_WIDTH), F32) + b_ref[...]
    for b in range(8):
        pb = None
        for a in range((base + CONV_KERNEL - 1) // 8 + 1):
            k = 8 * a + b - base
            if 0 <= k < CONV_KERNEL:
                term = w_ref[k:k + 1, :] * buf[8 * a:8 * a + tm + 8, :]
                pb = term if pb is None else pb + term
        part[...] = pb
        acc = acc + part[b:b + tm, :]
    mu = jnp.mean(acc, axis=-1, keepdims=True)
    xc = acc - mu
    var = jnp.mean(xc * xc, axis=-1, keepdims=True)
    y = xc * lax.rsqrt(var + LN_EPS) * lnw_ref[...] + lnb_ref[...]
    o_ref[...] = y * _sigmoid(y)


def conformer_conv(hc, seq, p, l):
    t = hc.shape[0]
    tm = TOKEN_TILE
    nbh = tm // CONV_HALO
    row = pl.BlockSpec((tm, CONV_WIDTH), lambda i: (i, 0))
    vec = lambda a: a.reshape(1, -1).astype(F32)
    return pl.pallas_call(
        functools.partial(_conv_kernel, seq // tm),
        grid=(t // tm,),
        in_specs=[row,
                  pl.BlockSpec((CONV_HALO, CONV_WIDTH), lambda i: (jnp.maximum(i * nbh - 1, 0), 0)),
                  pl.BlockSpec((CONV_HALO, CONV_WIDTH),
                               lambda i: (jnp.minimum((i + 1) * nbh, t // CONV_HALO - 1), 0)),
                  _const_spec((CONV_KERNEL, CONV_WIDTH)), _const_spec((1, CONV_WIDTH)),
                  _const_spec((1, CONV_WIDTH)), _const_spec((1, CONV_WIDTH))],
        out_specs=row,
        out_shape=jax.ShapeDtypeStruct((t, CONV_WIDTH), F32),
        scratch_shapes=[pltpu.VMEM((tm + 2 * CONV_HALO, CONV_WIDTH), F32),
                        pltpu.VMEM((tm + 8, CONV_WIDTH), F32)],
        compiler_params=_cparams(("parallel",)),
        name="conformer_conv",
    )(hc, hc, hc, p['conv_dw_w'][l], vec(p['conv_dw_b'][l]), vec(p['conv_ln_w'][l]), vec(p['conv_ln_b'][l]))


def _merge_kernel(x_ref, ya_ref, o_ref, bonus_ref, g_ref, yc_ref, gate_ref, ones_ref, gnw_ref, gnb_ref,
                  wa_ref, wb_ref, wc_ref, wo_ref, out_ref):
    ones = ones_ref[...]
    o = o_ref[0].astype(F32) + o_ref[1].astype(F32)
    inv = 1.0 / RWKV_HEAD_DIM
    mu = _split_dot(o, ones) * inv
    oc = o - mu
    var = _split_dot(oc * oc, ones) * inv
    on = oc * lax.rsqrt(var + GN_EPS) * gnw_ref[...] + gnb_ref[...]
    yb = (on + bonus_ref[...]) * g_ref[...]
    m = gate_ref[:, 0:D_MODEL].astype(F32) * _bdot(ya_ref[...], wa_ref[...])
    m = m + gate_ref[:, D_MODEL:2 * D_MODEL].astype(F32) * _bdot(yb, wb_ref[...])
    m = m + gate_ref[:, 2 * D_MODEL:3 * D_MODEL].astype(F32) * _bdot(yc_ref[...], wc_ref[...])
    out_ref[...] = x_ref[...] + _bdot(m, wo_ref[...])


def merge(x, ya, o, bonus, g, yc, gates, p, l):
    t = x.shape[0]
    tm = TOKEN_TILE
    row = lambda w: pl.BlockSpec((tm, w), lambda i: (i, 0))
    vec = lambda a: a.reshape(1, -1).astype(F32)
    return pl.pallas_call(
        _merge_kernel,
        grid=(t // tm,),
        in_specs=[row(D_MODEL), row(FNET_WIDTH),
                  pl.BlockSpec((N_DIR, tm, RWKV_WIDTH), lambda i: (0, i, 0)),
                  row(RWKV_WIDTH), row(RWKV_WIDTH), row(CONV_WIDTH), row(3 * D_MODEL),
                  _const_spec((RWKV_WIDTH, RWKV_WIDTH)), _const_spec((1, RWKV_WIDTH)), _const_spec((1, RWKV_WIDTH)),
                  _const_spec((FNET_WIDTH, D_MODEL)), _const_spec((RWKV_WIDTH, D_MODEL)),
                  _const_spec((CONV_WIDTH, D_MODEL)), _const_spec((D_MODEL, D_MODEL))],
        out_specs=row(D_MODEL),
        out_shape=jax.ShapeDtypeStruct((t, D_MODEL), F32),
        compiler_params=_cparams(("parallel",)),
        name="merge",
    )(x, ya, o, bonus, g, yc, gates, _head_ones(), vec(p['rwkv_gn_w'][l]), vec(p['rwkv_gn_b'][l]),
      p['fnet_w'][l].astype(BF), p['rwkv_w_o'][l].astype(BF), p['conv_w_o'][l].astype(BF),
      p['mix_w_out'][l].astype(BF))


FFN_HALO = 16


def _erf(x):
    return lax.erf(x)


def _ffn_kernel(tiles_per_seq, final, x_ref, xp_ref, xn_ref, g_ref, wup_ref, dww_ref, dwb_ref, wdn_ref, gf_ref,
                out_ref, xn_s, h_s):
    tm = x_ref.shape[0]
    i = pl.program_id(0)
    first = (i % tiles_per_seq) == 0
    last = (i % tiles_per_seq) == tiles_per_seq - 1
    g = g_ref[...]
    hl = FFN_HALO
    xn_s[0:hl] = _rms(xp_ref[...], g).astype(BF)
    xn_s[hl:tm + hl] = _rms(x_ref[...], g).astype(BF)
    xn_s[tm + hl:tm + 2 * hl] = _rms(xn_ref[...], g).astype(BF)
    rows = lax.broadcasted_iota(jnp.int32, (tm + 2 * hl, 1), 0)
    pad = (first & (rows < hl)) | (last & (rows >= tm + hl))
    acc = jnp.zeros((tm, D_MODEL), F32)
    for lo, hi in FF_BLOCKS:
        h = jnp.dot(xn_s[...], wup_ref[:, lo:hi], preferred_element_type=F32)
        h_s[:, 0:hi - lo] = jnp.where(pad, 0.0, h)
        hc = (dww_ref[0:1, lo:hi] * h_s[hl - 1:tm + hl - 1, 0:hi - lo]
              + dww_ref[1:2, lo:hi] * h_s[hl:tm + hl, 0:hi - lo]
              + dww_ref[2:3, lo:hi] * h_s[hl + 1:tm + hl + 1, 0:hi - lo] + dwb_ref[:, lo:hi])
        gate = jnp.dot(xn_s[hl:tm + hl], wup_ref[:, D_FF + lo:D_FF + hi], preferred_element_type=F32)
        act = 0.5 * hc * (1.0 + _erf(hc * (1.0 / math.sqrt(2.0)))) * gate
        acc = acc + jnp.dot(act.astype(BF), wdn_ref[lo:hi, :], preferred_element_type=F32)
    y = x_ref[...] + acc
    if final:
        y = _rms(y, gf_ref[...])
    out_ref[...] = y


def ffn(x, seq, p, l, final):
    t = x.shape[0]
    tm = TOKEN_TILE
    hl = FFN_HALO
    nbh = tm // hl
    row = pl.BlockSpec((tm, D_MODEL), lambda i: (i, 0))
    vec = lambda a: a.reshape(1, -1).astype(F32)
    fb = max(hi - lo for lo, hi in FF_BLOCKS)
    return pl.pallas_call(
        functools.partial(_ffn_kernel, seq // tm, final),
        grid=(t // tm,),
        in_specs=[row,
                  pl.BlockSpec((hl, D_MODEL), lambda i: (jnp.maximum(i * nbh - 1, 0), 0)),
                  pl.BlockSpec((hl, D_MODEL), lambda i: (jnp.minimum((i + 1) * nbh, t // hl - 1), 0)),
                  _const_spec((1, D_MODEL)), _const_spec((D_MODEL, 2 * D_FF)),
                  _const_spec((3, D_FF)), _const_spec((1, D_FF)), _const_spec((D_FF, D_MODEL)),
                  _const_spec((1, D_MODEL))],
        out_specs=row,
        out_shape=jax.ShapeDtypeStruct((t, D_MODEL), F32),
        scratch_shapes=[pltpu.VMEM((tm + 2 * hl, D_MODEL), BF), pltpu.VMEM((tm + 2 * hl, fb), F32)],
        compiler_params=_cparams(("parallel",)),
        name="ffn",
    )(x, x, x, vec(p['ffn_norm_g'][l]), p['ffn_w_up'][l].astype(BF), p['ffn_dw_w'][l],
      vec(p['ffn_dw_b'][l]), p['ffn_w_down'][l].astype(BF), vec(p['final_norm_g']))


def encoder(x3, p):
    bsz, seq, _ = x3.shape
    depth = p['w_in'].shape[0]
    x = x3.reshape(bsz * seq, D_MODEL)
    tables = _fft_tables(seq)
    for l in range(depth):
        ua, ub, hc, gates = mixer_in(x, p['attn_norm_g'][l], p['w_in'][l].astype(BF))
        ya = fourier_mix(ua, bsz, seq, tables)
        r, kk, v, krep, kka, logw, bonus, g = rwkv_prep(ub, seq, p, l)
        o = rwkv_scan(r, kk, v, krep, kka, logw, bsz, seq)
        yc = conformer_conv(hc, seq, p, l)
        x = merge(x, ya, o, bonus, g, yc, gates, p, l)
        x = ffn(x, seq, p, l, final=(l == depth - 1))
    return x.reshape(bsz, seq, D_MODEL)


def kernel(x_prompt, x_sample, attn_norm_g, w_in, fnet_w, rwkv_mu_prev, rwkv_mu_next, rwkv_w0, rwkv_w_up,
           rwkv_a0, rwkv_a_up, rwkv_g_up, rwkv_k_k, rwkv_k_a, rwkv_r_k, rwkv_gn_w, rwkv_gn_b, rwkv_w_o,
           conv_dw_w, conv_dw_b, conv_ln_w, conv_ln_b, conv_w_o, mix_w_out, ffn_norm_g, ffn_w_up,
           ffn_dw_w, ffn_dw_b, ffn_w_down, final_norm_g):
    p = dict(attn_norm_g=attn_norm_g, w_in=w_in, fnet_w=fnet_w, rwkv_mu_prev=rwkv_mu_prev,
             rwkv_mu_next=rwkv_mu_next, rwkv_w0=rwkv_w0, rwkv_w_up=rwkv_w_up, rwkv_a0=rwkv_a0,
             rwkv_a_up=rwkv_a_up, rwkv_g_up=rwkv_g_up, rwkv_k_k=rwkv_k_k, rwkv_k_a=rwkv_k_a, rwkv_r_k=rwkv_r_k,
             rwkv_gn_w=rwkv_gn_w, rwkv_gn_b=rwkv_gn_b, rwkv_w_o=rwkv_w_o, conv_dw_w=conv_dw_w,
             conv_dw_b=conv_dw_b, conv_ln_w=conv_ln_w, conv_ln_b=conv_ln_b, conv_w_o=conv_w_o,
             mix_w_out=mix_w_out, ffn_norm_g=ffn_norm_g, ffn_w_up=ffn_w_up, ffn_dw_w=ffn_dw_w,
             ffn_dw_b=ffn_dw_b, ffn_w_down=ffn_w_down, final_norm_g=final_norm_g)
    return (encoder(x_prompt, p), encoder(x_sample, p))
```

```python
import functools
import math

import numpy as np
import jax
import jax.numpy as jnp
from jax import lax
from jax.experimental import pallas as pl
from jax.experimental.pallas import tpu as pltpu

D_MODEL = 1024
FNET_GROUPS = 4
FNET_GROUP_DIM = 64
FNET_WIDTH = 256
RWKV_HEADS = 8
RWKV_HEAD_DIM = 64
RWKV_WIDTH = 512
N_DIR = 2
W_LORA = 64
A_LORA = 64
G_LORA = 128
RWKV_IN = 1920
CONV_WIDTH = 256
CONV_KERNEL = 31
IN_COLS = 5760
D_FF = 2816
RMS_EPS = 1e-6
LN_EPS = 1e-5
GN_EPS = 64e-5
DECAY_SCALE = math.exp(-0.5)

COL_A = 0
COL_B = FNET_WIDTH
COL_C = COL_B + RWKV_IN
COL_G = COL_C + 2 * CONV_WIDTH

BF = jnp.bfloat16
F32 = jnp.float32

TOKEN_TILE = 512
CHUNK = 64
FFT_N2 = 64
FFT_COLS = 2048
FFT_K1_BLOCK = 8
VMEM_LIMIT = 56 * 1024 * 1024
FF_BLOCKS = ((0, 1024), (1024, 2048), (2048, 2816))


def _cparams(sem):
    return pltpu.CompilerParams(dimension_semantics=sem, vmem_limit_bytes=VMEM_LIMIT)


def _bdot(a, b):
    return jnp.dot(a.astype(BF), b.astype(BF), preferred_element_type=F32)


def _split_dot(a, b_exact):
    ah = a.astype(BF)
    al = (a - ah.astype(F32)).astype(BF)
    return (jnp.dot(ah, b_exact, preferred_element_type=F32)
            + jnp.dot(al, b_exact, preferred_element_type=F32))


def _dot_t0(a, b):
    return lax.dot_general(a, b, (((0,), (0,)), ((), ())), preferred_element_type=F32)


def _dot_t1(a, b):
    return lax.dot_general(a, b, (((1,), (1,)), ((), ())), preferred_element_type=F32)


def _sigmoid(x):
    return 0.5 * jnp.tanh(0.5 * x) + 0.5


def _rms(x, g):
    return x * lax.rsqrt(jnp.mean(x * x, axis=-1, keepdims=True) + RMS_EPS) * g


def _const_spec(shape):
    nd = len(shape)
    return pl.BlockSpec(shape, lambda *_: (0,) * nd)


def _mixer_in_kernel(x_ref, g_ref, w_ref, ua_ref, ub_ref, hc_ref, gate_ref):
    xn = _rms(x_ref[...], g_ref[...]).astype(BF)
    ua_ref[...] = jnp.dot(xn, w_ref[:, COL_A:COL_B], preferred_element_type=F32).astype(BF)
    for lo, hi in ((0, 1024), (1024, RWKV_IN)):
        ub_ref[:, lo:hi] = jnp.dot(xn, w_ref[:, COL_B + lo:COL_B + hi], preferred_element_type=F32).astype(BF)
    uc = jnp.dot(xn, w_ref[:, COL_C:COL_G], preferred_element_type=F32)
    hc_ref[...] = uc[:, :CONV_WIDTH] * _sigmoid(uc[:, CONV_WIDTH:])
    for j in range(3):
        lo = j * D_MODEL
        ug = jnp.dot(xn, w_ref[:, COL_G + lo:COL_G + lo + D_MODEL], preferred_element_type=F32)
        gate_ref[:, lo:lo + D_MODEL] = _sigmoid(ug).astype(BF)


def mixer_in(x, g, w_in_bf):
    t = x.shape[0]
    tm = TOKEN_TILE
    row = lambda w: pl.BlockSpec((tm, w), lambda i: (i, 0))
    return pl.pallas_call(
        _mixer_in_kernel,
        grid=(t // tm,),
        in_specs=[row(D_MODEL), _const_spec((1, D_MODEL)), _const_spec((D_MODEL, IN_COLS))],
        out_specs=[row(FNET_WIDTH), row(RWKV_IN), row(CONV_WIDTH), row(3 * D_MODEL)],
        out_shape=[jax.ShapeDtypeStruct((t, FNET_WIDTH), BF), jax.ShapeDtypeStruct((t, RWKV_IN), BF),
                   jax.ShapeDtypeStruct((t, CONV_WIDTH), F32), jax.ShapeDtypeStruct((t, 3 * D_MODEL), BF)],
        compiler_params=_cparams(("parallel",)),
        name="mixer_in",
    )(x, g.reshape(1, D_MODEL), w_in_bf)


def _fft_tables(seq):
    n2 = FFT_N2
    n1 = seq // n2
    k1 = np.arange(n1)[:, None].astype(np.float64)
    m1 = np.arange(n1)[None, :].astype(np.float64)
    ang1 = 2.0 * np.pi * ((k1 * m1) % n1) / n1
    f1 = np.concatenate([np.cos(ang1), -np.sin(ang1)], axis=0)
    m2 = np.arange(n2)[None, :].astype(np.float64)
    angt = 2.0 * np.pi * ((k1 * m2) % seq) / seq
    tr, ti = np.cos(angt), -np.sin(angt)
    k2 = np.arange(n2)[:, None].astype(np.float64)
    ang2 = 2.0 * np.pi * ((k2 * m2) % n2) / n2
    c2, s2 = np.cos(ang2), np.sin(ang2)
    f2 = np.block([[c2, s2], [-s2, c2]])
    q = np.arange(FNET_GROUP_DIM)
    angc = 2.0 * np.pi * ((q[:, None] * q[None, :]) % FNET_GROUP_DIM) / FNET_GROUP_DIM
    scale = 1.0 / math.sqrt(seq * FNET_GROUP_DIM)
    eye = np.eye(FNET_GROUPS)
    cd = np.concatenate([np.kron(eye, np.cos(angc)), np.kron(eye, np.sin(angc))], axis=0) * scale
    tr = jnp.repeat(jnp.asarray(tr, F32), FNET_WIDTH, axis=1)
    ti = jnp.repeat(jnp.asarray(ti, F32), FNET_WIDTH, axis=1)
    return (jnp.asarray(f1, BF), tr, ti, jnp.asarray(f2, BF), jnp.asarray(cd, BF))


def _fft1_kernel(x_ref, f1_ref, tr_ref, ti_ref, o_ref):
    n1 = x_ref.shape[1]
    res = jnp.dot(f1_ref[...], x_ref[0].astype(BF), preferred_element_type=F32)
    ar, ai = res[:n1], res[n1:]
    tr, ti = tr_ref[...], ti_ref[...]
    o_ref[0, 0] = ar * tr - ai * ti
    o_ref[0, 1] = ar * ti + ai * tr


def _fft2_kernel(z_ref, f2_ref, cd_ref, o_ref):
    n2 = FFT_N2
    kb = FFT_K1_BLOCK
    z = jnp.concatenate([jnp.concatenate([z_ref[0, 0, i], z_ref[0, 1, i]], axis=0) for i in range(kb)],
                        axis=1).astype(BF)
    g = jnp.dot(f2_ref[...], z, preferred_element_type=F32).astype(BF)
    gc = jnp.concatenate(
        [jnp.concatenate([g[:n2, i * FNET_WIDTH:(i + 1) * FNET_WIDTH], g[n2:, i * FNET_WIDTH:(i + 1) * FNET_WIDTH]],
                         axis=1) for i in range(kb)], axis=0)
    y = jnp.dot(gc, cd_ref[...], preferred_element_type=F32)
    for i in range(kb):
        o_ref[0, :, i * FNET_WIDTH:(i + 1) * FNET_WIDTH] = y[i * n2:(i + 1) * n2]


def fourier_mix(ua, bsz, seq, tables):
    f1, tr, ti, f2, cd = tables
    n2 = FFT_N2
    n1 = seq // n2
    cols = n2 * FNET_WIDTH
    cb = FFT_COLS
    x = ua.reshape(bsz, n1, cols)
    z = pl.pallas_call(
        _fft1_kernel,
        grid=(bsz, cols // cb),
        in_specs=[pl.BlockSpec((1, n1, cb), lambda b, j: (b, 0, j)),
                  _const_spec((2 * n1, n1)),
                  pl.BlockSpec((n1, cb), lambda b, j: (0, j)),
                  pl.BlockSpec((n1, cb), lambda b, j: (0, j))],
        out_specs=pl.BlockSpec((1, 2, n1, cb), lambda b, j: (b, 0, 0, j)),
        out_shape=jax.ShapeDtypeStruct((bsz, 2, n1, cols), F32),
        compiler_params=_cparams(("parallel", "parallel")),
        name="fft1",
    )(x, f1, tr, ti)
    z = z.reshape(bsz, 2, n1, n2, FNET_WIDTH)
    kb = FFT_K1_BLOCK
    y = pl.pallas_call(
        _fft2_kernel,
        grid=(bsz, n1 // kb),
        in_specs=[pl.BlockSpec((1, 2, kb, n2, FNET_WIDTH), lambda b, j: (b, 0, j, 0, 0)),
                  _const_spec((2 * n2, 2 * n2)),
                  _const_spec((2 * FNET_WIDTH, FNET_WIDTH))],
        out_specs=pl.BlockSpec((1, n2, kb * FNET_WIDTH), lambda b, j: (b, 0, j)),
        out_shape=jax.ShapeDtypeStruct((bsz, n2, n1 * FNET_WIDTH), F32),
        compiler_params=_cparams(("parallel", "parallel")),
        name="fft2",
    )(z, f2, cd)
    return y.reshape(bsz * seq, FNET_WIDTH)


PREP_HALO = 16


def _head_ones():
    h = np.arange(RWKV_WIDTH) // RWKV_HEAD_DIM
    return jnp.asarray((h[:, None] == h[None, :]).astype(np.float32), BF)


def _rwkv_prep_kernel(tiles_per_seq, u_ref, up_ref, un_ref, mup_ref, mun_ref, w0_ref, wup_ref, a0_ref, aup_ref,
                      gup_ref, kk_w_ref, ka_ref, rk_ref, ones_ref,
                      r_ref, kkn_ref, v_ref, krep_ref, kka_ref, logw_ref, bonus_ref, g_ref, buf):
    tm = u_ref.shape[0]
    i = pl.program_id(0)
    first = (i % tiles_per_seq) == 0
    last = (i % tiles_per_seq) == tiles_per_seq - 1
    buf[0:8] = jnp.where(first, 0.0, up_ref[...].astype(F32)[PREP_HALO - 8:])
    buf[8:tm + 8] = u_ref[...].astype(F32)
    buf[tm + 8:tm + 16] = jnp.where(last, 0.0, un_ref[...].astype(F32)[:8])
    mup, mun = mup_ref[...], mun_ref[...]

    def shifted(lo, hi):
        u = buf[8:tm + 8, lo:hi]
        return (u + mup[:, lo:hi] * (buf[7:tm + 7, lo:hi] - u) + mun[:, lo:hi] * (buf[9:tm + 9, lo:hi] - u))

    c0 = RWKV_WIDTH
    r = shifted(0, c0)
    k = shifted(c0, 2 * c0)
    v = shifted(2 * c0, 3 * c0)
    wd = shifted(3 * c0, 3 * c0 + 128)
    ad = shifted(3 * c0 + 128, 3 * c0 + 256)
    gd = shifted(3 * c0 + 256, 3 * c0 + 384)

    ones = ones_ref[...]
    r_ref[...] = r.astype(BF)
    v_ref[...] = v.astype(BF)
    kk = k * kk_w_ref[...]
    ss = _split_dot(kk * kk, ones)
    kk = kk * lax.rsqrt(jnp.maximum(ss, 1e-24))
    kkn_ref[...] = kk.astype(BF)
    bonus_ref[...] = (_split_dot(r * k * rk_ref[...], ones) * v).astype(BF)
    g_ref[...] = _bdot(_sigmoid(gd), gup_ref[...]).astype(BF)

    wlogit = _bdot(jnp.tanh(wd), wup_ref[...])
    alogit = _bdot(ad, aup_ref[...])
    ka = ka_ref[...]
    for d in range(N_DIR):
        sl = slice(d * c0, (d + 1) * c0)
        logw_ref[d] = -DECAY_SCALE * _sigmoid(w0_ref[:, sl] + wlogit[:, sl])
        a = _sigmoid(a0_ref[:, sl] + alogit[:, sl])
        krep_ref[d] = (k * (1.0 + (a - 1.0) * ka)).astype(BF)
        kka_ref[d] = (kk * a).astype(BF)


def _lora_block(w):
    z = jnp.zeros_like(w[0])
    return jnp.concatenate([jnp.concatenate([w[0], z], axis=1), jnp.concatenate([z, w[1]], axis=1)], axis=0)


def rwkv_prep(ub, seq, p, l):
    t = ub.shape[0]
    tm = TOKEN_TILE
    nbh = tm // PREP_HALO
    row = lambda w: pl.BlockSpec((tm, w), lambda i: (i, 0))
    row2 =pl.BlockSpec((N_DIR, tm, RWKV_WIDTH), lambda i: (0, i, 0))
    vec = lambda a: a.reshape(1, -1).astype(F32)
    sds = jax.ShapeDtypeStruct
    outs = pl.pallas_call(
        functools.partial(_rwkv_prep_kernel, seq // tm),
        grid=(t // tm,),
        in_specs=[row(RWKV_IN),
                  pl.BlockSpec((PREP_HALO, RWKV_IN), lambda i: (jnp.maximum(i * nbh - 1, 0), 0)),
                  pl.BlockSpec((PREP_HALO, RWKV_IN), lambda i: (jnp.minimum((i + 1) * nbh, t // PREP_HALO - 1), 0)),
                  _const_spec((1, RWKV_IN)), _const_spec((1, RWKV_IN)),
                  _const_spec((1, 2 * RWKV_WIDTH)), _const_spec((2 * W_LORA, 2 * RWKV_WIDTH)),
                  _const_spec((1, 2 * RWKV_WIDTH)), _const_spec((2 * A_LORA, 2 * RWKV_WIDTH)),
                  _const_spec((G_LORA, RWKV_WIDTH)),
                  _const_spec((1, RWKV_WIDTH)), _const_spec((1, RWKV_WIDTH)), _const_spec((1, RWKV_WIDTH)),
                  _const_spec((RWKV_WIDTH, RWKV_WIDTH))],
        out_specs=[row(RWKV_WIDTH), row(RWKV_WIDTH), row(RWKV_WIDTH), row2, row2, row2,
                   row(RWKV_WIDTH), row(RWKV_WIDTH)],
        out_shape=[sds((t, RWKV_WIDTH), BF)] * 3 + [sds((N_DIR, t, RWKV_WIDTH), BF)] * 2
                  + [sds((N_DIR, t, RWKV_WIDTH), F32)] + [sds((t, RWKV_WIDTH), BF)] * 2,
        scratch_shapes=[pltpu.VMEM((tm + 16, RWKV_IN), F32)],
        compiler_params=_cparams(("parallel",)),
        name="rwkv_prep",
    )(ub, ub, ub, vec(p['rwkv_mu_prev'][l]), vec(p['rwkv_mu_next'][l]),
      vec(p['rwkv_w0'][l]), _lora_block(p['rwkv_w_up'][l]).astype(BF),
      vec(p['rwkv_a0'][l]), _lora_block(p['rwkv_a_up'][l]).astype(BF),
      p['rwkv_g_up'][l].astype(BF), vec(p['rwkv_k_k'][l]), vec(p['rwkv_k_a'][l]), vec(p['rwkv_r_k'][l]),
      _head_ones())
    return outs


GROUP_HEADS = 4
GROUP = GROUP_HEADS * RWKV_HEAD_DIM
N_GROUP = RWKV_HEADS // GROUP_HEADS
SCAN_MAX_ITERS = 4
SCAN_BATCH = 8
SCAN_TOKENS = 1024


def _scan_kernel(r_ref, kk_ref, v_ref, krep_ref, kka_ref, logw_ref, o_ref, s_ref, *scratch):
    L = CHUNK
    nb = r_ref.shape[0]
    nch = r_ref.shape[1] // L
    n_iter = min(SCAN_MAX_ITERS, nch)
    cpi = nb * nch // n_iter
    steps_per_iter = nch // n_iter
    n_yield = 4 * steps_per_iter
    d = pl.program_id(1)
    step = pl.program_id(2)
    half = len(scratch) // 2
    sets = (scratch[:half], scratch[half:])

    @pl.when(step == 0)
    def _():
        s_ref[...] = jnp.zeros_like(s_ref)
        for ref in sets[1]:
            ref[...] = jnp.zeros_like(ref)

    sgn = 1 - 2 * d
    r64 = lax.broadcasted_iota(jnp.int32, (L, L), 0)
    c64 = lax.broadcasted_iota(jnp.int32, (L, L), 1)
    tri = jnp.where((r64 - c64) * sgn >= 0, 1.0, 0.0).astype(BF)
    row = lax.broadcasted_iota(jnp.int32, (L, GROUP), 0)
    sidx = lax.broadcasted_iota(jnp.int32, (L, GROUP), 1) % L
    diff = (row - sidx) * sgn
    strict = diff > 0
    incl = diff >= 0
    eye = jnp.where(row == sidx, 1.0, 0.0)
    brow = lax.broadcasted_iota(jnp.int32, (GROUP, GROUP), 0) // L
    bcol = lax.broadcasted_iota(jnp.int32, (GROUP, GROUP), 1) // L
    same_head = brow == bcol
    lane128 = lax.broadcasted_iota(jnp.int32, (L, 128), 1)
    half_ones = [jnp.where(lane128 < L, 1.0, 0.0).astype(BF), jnp.where(lane128 >= L, 1.0, 0.0).astype(BF)]
    zero_tile = jnp.zeros((L, 128), BF)

    def bdiag(x):
        xb = x.astype(BF)
        blocks = []
        for h in range(GROUP_HEADS):
            tile = xb[:, 128 * (h // 2):128 * (h // 2 + 1)] * half_ones[h % 2]
            blocks.append(jnp.concatenate([tile, zero_tile] if h < 2 else [zero_tile, tile], axis=1))
        return jnp.concatenate(blocks, axis=0)

    def off_mask(b):
        return ((row // (2 * b)) == (sidx // (2 * b))) & ((row // b) != (sidx // b))

    def stage2(it, src):
        ar_s, t_s, mrb_s, x0_s, o0_s, bk_s, v_s, dec_s = src
        ids = [(bi, gi) for bi in range(nb) for gi in range(N_GROUP)]
        gsl = [slice(gi * GROUP, (gi + 1) * GROUP) for gi in range(N_GROUP)]
        for q in range(steps_per_iter):
            jj = it * steps_per_iter + q
            j = jj + d * (nch - 1 - 2 * jj)
            sl = pl.ds(pl.multiple_of(j * L, L), L)
            s = {k: s_ref[k] for k in ids}
            arh = {k: _dot_t1(ar_s[k[0], j, k[1]], s[k].astype(BF)) for k in ids}
            yield
            u = {k: jnp.dot(t_s[k[0], j, k[1]], bdiag(arh[k][:L] + x0_s[k[0], j, k[1]]),
                            preferred_element_type=F32) for k in ids}
            yield
            upd = {}
            for bi, gi in ids:
                uv = jnp.concatenate([u[(bi, gi)].astype(BF), v_s[bi, j][:, gsl[gi]]], axis=0)
                upd[(bi, gi)] = _dot_t0(uv, bk_s[bi, j, gi])
            for bi, gi in ids:
                dec = dec_s[bi, j]
                s_ref[bi, gi] = s[(bi, gi)] * dec[0:1, gsl[gi]] + jnp.where(same_head, upd[(bi, gi)], 0.0)
            yield
            for bi, gi in ids:
                o = (arh[(bi, gi)][L:] + o0_s[bi, j, gi]
                     - jnp.dot(mrb_s[bi, j, gi], bdiag(u[(bi, gi)]), preferred_element_type=F32))
                o_ref[0, bi, sl, gsl[gi]] = o.astype(BF)
            yield

    def body(it, dst, src):
        ar_s, t_s, mrb_s, x0_s, o0_s, bk_s, v_s, dec_s = dst
        other = stage2(it, src)
        n_slots = 14

        def tick(slot):
            for _ in range((n_yield * (slot + 1)) // n_slots - (n_yield * slot) // n_slots):
                next(other, None)

        chains = []
        for q in range(cpi):
            cid = it * cpi + q
            bi = cid // nch
            j = cid % nch
            sl = pl.ds(pl.multiple_of(j * L, L), L)
            lw = logw_ref[0, bi, sl, :]
            lwh = lw.astype(BF)
            lwl = (lw - lwh.astype(F32)).astype(BF)
            g = jnp.dot(tri, lwh, preferred_element_type=F32) + jnp.dot(tri, lwl, preferred_element_type=F32)
            gtot = jnp.sum(lw, axis=0, keepdims=True)
            e_e = jnp.exp(g - lw)
            e_mg = jnp.exp(-g)
            e_gt = jnp.exp(gtot - g)
            kk = kk_ref[bi, sl, :].astype(F32)
            kka = kka_ref[0, bi, sl, :].astype(F32)
            krep = krep_ref[0, bi, sl, :].astype(F32)
            at = kk * e_e
            bh = kka * e_mg
            kh = krep * e_mg
            rh = r_ref[bi, sl, :].astype(F32) * jnp.exp(g)
            bg = kka * e_gt
            kg = krep * e_gt
            vv = v_ref[bi, sl, :]
            dec_s[bi, j] = jnp.broadcast_to(jnp.exp(gtot), (8, RWKV_WIDTH))
            v_s[bi, j] = vv
            for gi in range(N_GROUP):
                gs = slice(gi * GROUP, (gi + 1) * GROUP)
                ar = jnp.concatenate([at[:, gs], rh[:, gs]], axis=0).astype(BF)
                ar_s[bi, j, gi] = ar
                bk_s[bi, j, gi] = jnp.concatenate([-bg[:, gs], kg[:, gs]], axis=0).astype(BF)
                chains.append(dict(ix=(bi, j, gi), ar=ar, bh=bh[:, gs], kh=kh[:, gs], v=vv[:, gs]))
        tick(0)
        for ch in chains:
            ch['nb'] = _dot_t1(ch['ar'], bdiag(ch['bh']))
        tick(1)
        for ch in chains:
            ch['nk'] = _dot_t1(ch['ar'], bdiag(ch['kh']))
        tick(2)
        for ch in chains:
            nk = ch['nk']
            lhs = jnp.concatenate([jnp.where(strict, nk[:L], 0.0), jnp.where(incl, nk[L:], 0.0)], axis=0)
            xo = jnp.dot(lhs.astype(BF), bdiag(ch['v']), preferred_element_type=F32)
            x0_s[ch['ix']] = xo[:L]
            o0_s[ch['ix']] = xo[L:]
        tick(3)
        for ch in chains:
            nbm = ch['nb']
            mrb_s[ch['ix']] = jnp.where(incl, nbm[L:], 0.0).astype(BF)
            ch['n'] = jnp.where(strict, nbm[:L], 0.0)
            ch['t'] = eye - jnp.where(off_mask(1), ch['n'], 0.0)
        slot = 4
        b = 2
        while b < L:
            om = off_mask(b)
            for ch in chains:
                ch['p'] = jnp.dot(ch['t'].astype(BF), bdiag(jnp.where(om, ch['n'], 0.0)),
                                  preferred_element_type=F32)
            tick(slot)
            for ch in chains:
                ch['t'] = ch['t'] - jnp.dot(ch['p'].astype(BF), bdiag(ch['t']), preferred_element_type=F32)
            tick(slot + 1)
            slot += 2
            b *= 2
        assert slot == n_slots
        for ch in chains:
            t_s[ch['ix']] = ch['t'].astype(BF)
        for _ in other:
            pass

    for par in range(2):
        @pl.when(step % 2 == par)
        def _(par=par):
            def loop_body(it, carry):
                body(it, sets[par], sets[1 - par])
                return carry
            lax.fori_loop(0, n_iter, loop_body, 0)


def rwkv_scan(r, kk, v, krep, kka, logw, bsz, seq):
    t = r.shape[0]
    nb = min(SCAN_BATCH, bsz)
    ts = SCAN_TOKENS // nb
    nc = seq // ts
    nch = ts // CHUNK

    def tile_in(d, c):
        c = jnp.minimum(c, nc - 1)
        return c + d * (nc - 1 - 2 * c)

    def tile_out(d, c):
        c = jnp.maximum(c - 1, 0)
        return c + d * (nc - 1 - 2 * c)

    shared = pl.BlockSpec((nb, ts, RWKV_WIDTH), lambda b, d, c: (b, tile_in(d, c), 0))
    perdir = pl.BlockSpec((1, nb, ts, RWKV_WIDTH), lambda b, d, c: (d, b, tile_in(d, c), 0))
    outspec = pl.BlockSpec((1, nb, ts, RWKV_WIDTH), lambda b, d, c: (d, b, tile_out(d, c), 0))
    shape3 = (bsz, seq, RWKV_WIDTH)
    shape4 = (N_DIR, bsz, seq, RWKV_WIDTH)
    per = (nb, nch, N_GROUP)
    factor_set = [pltpu.VMEM(per + (2 * CHUNK, GROUP), BF),
                  pltpu.VMEM(per + (CHUNK, GROUP), BF),
                  pltpu.VMEM(per + (CHUNK, GROUP), BF),
                  pltpu.VMEM(per + (CHUNK, GROUP), F32),
                  pltpu.VMEM(per + (CHUNK, GROUP), F32),
                  pltpu.VMEM(per + (2 * CHUNK, GROUP), BF),
                  pltpu.VMEM((nb, nch, CHUNK, RWKV_WIDTH), BF),
                  pltpu.VMEM((nb, nch, 8, RWKV_WIDTH), F32)]
    o = pl.pallas_call(
        _scan_kernel,
        grid=(bsz // nb, N_DIR, nc + 1),
        in_specs=[shared, shared, shared, perdir, perdir, perdir],
        out_specs=outspec,
        out_shape=jax.ShapeDtypeStruct(shape4, BF),
        scratch_shapes=[pltpu.VMEM((nb, N_GROUP, GROUP, GROUP), F32)] + factor_set + factor_set,
        compiler_params=_cparams(("parallel", "arbitrary", "arbitrary")),
        name="rwkv_scan",
    )(r.reshape(shape3), kk.reshape(shape3), v.reshape(shape3), krep.reshape(shape4), kka.reshape(shape4),
      logw.reshape(shape4))
    return o.reshape(N_DIR, t, RWKV_WIDTH)


CONV_HALO = 16


def _conv_kernel(tiles_per_seq, h_ref, hp_ref, hn_ref, w_ref, b_ref, lnw_ref, lnb_ref, o_ref, buf, part):
    tm = h_ref.shape[0]
    i = pl.program_id(0)
    first = (i % tiles_per_seq) == 0
    last = (i % tiles_per_seq) == tiles_per_seq - 1
    buf[0:CONV_HALO] = jnp.where(first, 0.0, hp_ref[...])
    buf[CONV_HALO:tm + CONV_HALO] = h_ref[...]
    buf[tm + CONV_HALO:tm + 2 * CONV_HALO] = jnp.where(last, 0.0, hn_ref[...])
    base = CONV_HALO - CONV_KERNEL // 2
    acc = jnp.zeros((tm, CONV_WIDTH), F32) + b_ref[...]
    for b in range(8):
        pb = None
        for a in range((base + CONV_KERNEL - 1) // 8 + 1):
            k = 8 * a + b - base
            if 0 <= k < CONV_KERNEL:
                term = w_ref[k:k + 1, :] * buf[8 * a:8 * a + tm + 8, :]
                pb = term if pb is None else pb + term
        part[...] = pb
        acc = acc + part[b:b + tm, :]
    mu = jnp.mean(acc, axis=-1, keepdims=True)
    xc = acc - mu
    var = jnp.mean(xc * xc, axis=-1, keepdims=True)
    y = xc * lax.rsqrt(var + LN_EPS) * lnw_ref[...] + lnb_ref[...]
    o_ref[...] = y * _sigmoid(y)


def conformer_conv(hc, seq, p, l):
    t = hc.shape[0]
    tm = TOKEN_TILE
    nbh = tm // CONV_HALO
    row = pl.BlockSpec((tm, CONV_WIDTH), lambda i: (i, 0))
    vec = lambda a: a.reshape(1, -1).astype(F32)
    return pl.pallas_call(
        functools.partial(_conv_kernel, seq // tm),
        grid=(t // tm,),
        in_specs=[row,
                  pl.BlockSpec((CONV_HALO, CONV_WIDTH), lambda i: (jnp.maximum(i * nbh - 1, 0), 0)),
                  pl.BlockSpec((CONV_HALO, CONV_WIDTH),
                               lambda i: (jnp.minimum((i + 1) * nbh, t // CONV_HALO - 1), 0)),
                  _const_spec((CONV_KERNEL, CONV_WIDTH)), _const_spec((1, CONV_WIDTH)),
                  _const_spec((1, CONV_WIDTH)), _const_spec((1, CONV_WIDTH))],
        out_specs=row,
        out_shape=jax.ShapeDtypeStruct((t, CONV_WIDTH), F32),
        scratch_shapes=[pltpu.VMEM((tm + 2 * CONV_HALO, CONV_WIDTH), F32),
                        pltpu.VMEM((tm + 8, CONV_WIDTH), F32)],
        compiler_params=_cparams(("parallel",)),
        name="conformer_conv",
    )(hc, hc, hc, p['conv_dw_w'][l], vec(p['conv_dw_b'][l]), vec(p['conv_ln_w'][l]), vec(p['conv_ln_b'][l]))


def _merge_kernel(x_ref, ya_ref, o_ref, bonus_ref, g_ref, yc_ref, gate_ref, ones_ref, gnw_ref, gnb_ref,
                  wa_ref, wb_ref, wc_ref, wo_ref, out_ref):
    ones = ones_ref[...]
    o = o_ref[0].astype(F32) + o_ref[1].astype(F32)
    inv = 1.0 / RWKV_HEAD_DIM
    mu = _split_dot(o, ones) * inv
    oc = o - mu
    var = _split_dot(oc * oc, ones) * inv
    on = oc * lax.rsqrt(var + GN_EPS) * gnw_ref[...] + gnb_ref[...]
    yb = (on + bonus_ref[...]) * g_ref[...]
    m = gate_ref[:, 0:D_MODEL].astype(F32) * _bdot(ya_ref[...], wa_ref[...])
    m = m + gate_ref[:, D_MODEL:2 * D_MODEL].astype(F32) * _bdot(yb, wb_ref[...])
    m = m + gate_ref[:, 2 * D_MODEL:3 * D_MODEL].astype(F32) * _bdot(yc_ref[...], wc_ref[...])
    out_ref[...] = x_ref[...] + _bdot(m, wo_ref[...])


def merge(x, ya, o, bonus, g, yc, gates, p, l):
    t = x.shape[0]
    tm = TOKEN_TILE
    row = lambda w: pl.BlockSpec((tm, w), lambda i: (i, 0))
    vec = lambda a: a.reshape(1, -1).astype(F32)
    return pl.pallas_call(
        _merge_kernel,
        grid=(t // tm,),
        in_specs=[row(D_MODEL), row(FNET_WIDTH),
                  pl.BlockSpec((N_DIR, tm, RWKV_WIDTH), lambda i: (0, i, 0)),
                  row(RWKV_WIDTH), row(RWKV_WIDTH), row(CONV_WIDTH), row(3 * D_MODEL),
                  _const_spec((RWKV_WIDTH, RWKV_WIDTH)), _const_spec((1, RWKV_WIDTH)), _const_spec((1, RWKV_WIDTH)),
                  _const_spec((FNET_WIDTH, D_MODEL)), _const_spec((RWKV_WIDTH, D_MODEL)),
                  _const_spec((CONV_WIDTH, D_MODEL)), _const_spec((D_MODEL, D_MODEL))],
        out_specs=row(D_MODEL),
        out_shape=jax.ShapeDtypeStruct((t, D_MODEL), F32),
        compiler_params=_cparams(("parallel",)),
        name="merge",
    )(x, ya, o, bonus, g, yc, gates, _head_ones(), vec(p['rwkv_gn_w'][l]), vec(p['rwkv_gn_b'][l]),
      p['fnet_w'][l].astype(BF), p['rwkv_w_o'][l].astype(BF), p['conv_w_o'][l].astype(BF),
      p['mix_w_out'][l].astype(BF))


FFN_HALO = 16


def _erf(x):
    return lax.erf(x)


def _ffn_kernel(tiles_per_seq, final, x_ref, xp_ref, xn_ref, g_ref, wup_ref, dww_ref, dwb_ref, wdn_ref, gf_ref,
                out_ref, xn_s, h_s):
    tm = x_ref.shape[0]
    i = pl.program_id(0)
    first = (i % tiles_per_seq) == 0
    last = (i % tiles_per_seq) == tiles_per_seq - 1
    g = g_ref[...]
    hl = FFN_HALO
    xn_s[0:hl] = _rms(xp_ref[...], g).astype(BF)
    xn_s[hl:tm + hl] = _rms(x_ref[...], g).astype(BF)
    xn_s[tm + hl:tm + 2 * hl] = _rms(xn_ref[...], g).astype(BF)
    rows = lax.broadcasted_iota(jnp.int32, (tm + 2 * hl, 1), 0)
    pad = (first & (rows < hl)) | (last & (rows >= tm + hl))
    acc = jnp.zeros((tm, D_MODEL), F32)
    for lo, hi in FF_BLOCKS:
        h = jnp.dot(xn_s[...], wup_ref[:, lo:hi], preferred_element_type=F32)
        h_s[:, 0:hi - lo] = jnp.where(pad, 0.0, h)
        hc = (dww_ref[0:1, lo:hi] * h_s[hl - 1:tm + hl - 1, 0:hi - lo]
              + dww_ref[1:2, lo:hi] * h_s[hl:tm + hl, 0:hi - lo]
              + dww_ref[2:3, lo:hi] * h_s[hl + 1:tm + hl + 1, 0:hi - lo] + dwb_ref[:, lo:hi])
        gate = jnp.dot(xn_s[hl:tm + hl], wup_ref[:, D_FF + lo:D_FF + hi], preferred_element_type=F32)
        act = 0.5 * hc * (1.0 + _erf(hc * (1.0 / math.sqrt(2.0)))) * gate
        acc = acc + jnp.dot(act.astype(BF), wdn_ref[lo:hi, :], preferred_element_type=F32)
    y = x_ref[...] + acc
    if final:
        y = _rms(y, gf_ref[...])
    out_ref[...] = y


def ffn(x, seq, p, l, final):
    t = x.shape[0]
    tm = TOKEN_TILE
    hl = FFN_HALO
    nbh = tm // hl
    row = pl.BlockSpec((tm, D_MODEL), lambda i: (i, 0))
    vec = lambda a: a.reshape(1, -1).astype(F32)
    fb = max(hi - lo for lo, hi in FF_BLOCKS)
    return pl.pallas_call(
        functools.partial(_ffn_kernel, seq // tm, final),
        grid=(t // tm,),
        in_specs=[row,
                  pl.BlockSpec((hl, D_MODEL), lambda i: (jnp.maximum(i * nbh - 1, 0), 0)),
                  pl.BlockSpec((hl, D_MODEL), lambda i: (jnp.minimum((i + 1) * nbh, t // hl - 1), 0)),
                  _const_spec((1, D_MODEL)), _const_spec((D_MODEL, 2 * D_FF)),
                  _const_spec((3, D_FF)), _const_spec((1, D_FF)), _const_spec((D_FF, D_MODEL)),
                  _const_spec((1, D_MODEL))],
        out_specs=row,
        out_shape=jax.ShapeDtypeStruct((t, D_MODEL), F32),
        scratch_shapes=[pltpu.VMEM((tm + 2 * hl, D_MODEL), BF), pltpu.VMEM((tm + 2 * hl, fb), F32)],
        compiler_params=_cparams(("parallel",)),
        name="ffn",
    )(x, x, x, vec(p['ffn_norm_g'][l]), p['ffn_w_up'][l].astype(BF), p['ffn_dw_w'][l],
      vec(p['ffn_dw_b'][l]), p['ffn_w_down'][l].astype(BF), vec(p['final_norm_g']))


def encoder(x3, p):
    bsz, seq, _ = x3.shape
    depth = p['w_in'].shape[0]
    x = x3.reshape(bsz * seq, D_MODEL)
    tables = _fft_tables(seq)
    for l in range(depth):
        ua, ub, hc, gates = mixer_in(x, p['attn_norm_g'][l], p['w_in'][l].astype(BF))
        ya = fourier_mix(ua, bsz, seq, tables)
        r, kk, v, krep, kka, logw, bonus, g = rwkv_prep(ub, seq, p, l)
        o = rwkv_scan(r, kk, v, krep, kka, logw, bsz, seq)
        yc = conformer_conv(hc, seq, p, l)
        x = merge(x, ya, o, bonus, g, yc, gates, p, l)
        x = ffn(x, seq, p, l, final=(l == depth - 1))
    return x.reshape(bsz, seq, D_MODEL)


def kernel(x_prompt, x_sample, attn_norm_g, w_in, fnet_w, rwkv_mu_prev, rwkv_mu_next, rwkv_w0, rwkv_w_up,
           rwkv_a0, rwkv_a_up, rwkv_g_up, rwkv_k_k, rwkv_k_a, rwkv_r_k, rwkv_gn_w, rwkv_gn_b, rwkv_w_o,
           conv_dw_w, conv_dw_b, conv_ln_w, conv_ln_b, conv_w_o, mix_w_out, ffn_norm_g, ffn_w_up,
           ffn_dw_w, ffn_dw_b, ffn_w_down, final_norm_g):
    p = dict(attn_norm_g=attn_norm_g, w_in=w_in, fnet_w=fnet_w, rwkv_mu_prev=rwkv_mu_prev,
             rwkv_mu_next=rwkv_mu_next, rwkv_w0=rwkv_w0, rwkv_w_up=rwkv_w_up, rwkv_a0=rwkv_a0,
             rwkv_a_up=rwkv_a_up, rwkv_g_up=rwkv_g_up, rwkv_k_k=rwkv_k_k, rwkv_k_a=rwkv_k_a, rwkv_r_k=rwkv_r_k,
             rwkv_gn_w=rwkv_gn_w, rwkv_gn_b=rwkv_gn_b, rwkv_w_o=rwkv_w_o, conv_dw_w=conv_dw_w,
             conv_dw_b=conv_dw_b, conv_ln_w=conv_ln_w, conv_ln_b=conv_ln_b, conv_w_o=conv_w_o,
             mix_w_out=mix_w_out, ffn_norm_g=ffn_norm_g, ffn_w_up=ffn_w_up, ffn_dw_w=ffn_dw_w,
             ffn_dw_b=ffn_dw_b, ffn_w_down=ffn_w_down, final_norm_g=final_norm_g)
    return (encoder(x_prompt, p), encoder(x_sample, p))
```

```python
import functools
import math

import numpy as np
import jax
import jax.numpy as jnp
from jax import lax
from jax.experimental import pallas as pl
from jax.experimental.pallas import tpu as pltpu

D_MODEL = 1024
FNET_GROUPS = 4
FNET_GROUP_DIM = 64
FNET_WIDTH = 256
RWKV_HEADS = 8
RWKV_HEAD_DIM = 64
RWKV_WIDTH = 512
N_DIR = 2
W_LORA = 64
A_LORA = 64
G_LORA = 128
RWKV_IN = 1920
CONV_WIDTH = 256
CONV_KERNEL = 31
IN_COLS = 5760
D_FF = 2816
RMS_EPS = 1e-6
LN_EPS = 1e-5
GN_EPS = 64e-5
DECAY_SCALE = math.exp(-0.5)

COL_A = 0
COL_B = FNET_WIDTH
COL_C = COL_B + RWKV_IN
COL_G = COL_C + 2 * CONV_WIDTH

BF = jnp.bfloat16
F32 = jnp.float32

TOKEN_TILE = 512
CHUNK = 64
FFT_N2 = 64
FFT_COLS = 2048
FFT_K1_BLOCK = 8
VMEM_LIMIT = 56 * 1024 * 1024
FF_BLOCKS = ((0, 1024), (1024, 2048), (2048, 2816))


def _cparams(sem):
    return pltpu.CompilerParams(dimension_semantics=sem, vmem_limit_bytes=VMEM_LIMIT)


def _bdot(a, b):
    return jnp.dot(a.astype(BF), b.astype(BF), preferred_element_type=F32)


def _dot_t0(a, b):
    return lax.dot_general(a, b, (((0,), (0,)), ((), ())), preferred_element_type=F32)


def _dot_t1(a, b):
    return lax.dot_general(a, b, (((1,), (1,)), ((), ())), preferred_element_type=F32)


def _sigmoid(x):
    return 0.5 * jnp.tanh(0.5 * x) + 0.5


def _rms(x, g):
    return x * lax.rsqrt(jnp.mean(x * x, axis=-1, keepdims=True) + RMS_EPS) * g


def _const_spec(shape):
    nd = len(shape)
    return pl.BlockSpec(shape, lambda *_: (0,) * nd)


def _mixer_in_kernel(x_ref, g_ref, w_ref, ua_ref, ub_ref, hc_ref, gate_ref):
    xn = _rms(x_ref[...], g_ref[...]).astype(BF)
    ua_ref[...] = jnp.dot(xn, w_ref[:, COL_A:COL_B], preferred_element_type=F32).astype(BF)
    for lo, hi in ((0, 1024), (1024, RWKV_IN)):
        ub_ref[:, lo:hi] = jnp.dot(xn, w_ref[:, COL_B + lo:COL_B + hi], preferred_element_type=F32).astype(BF)
    uc = jnp.dot(xn, w_ref[:, COL_C:COL_G], preferred_element_type=F32)
    hc_ref[...] = uc[:, :CONV_WIDTH] * _sigmoid(uc[:, CONV_WIDTH:])
    for j in range(3):
        lo = j * D_MODEL
        ug = jnp.dot(xn, w_ref[:, COL_G + lo:COL_G + lo + D_MODEL], preferred_element_type=F32)
        gate_ref[:, lo:lo + D_MODEL] = _sigmoid(ug).astype(BF)


def mixer_in(x, g, w_in_bf):
    t = x.shape[0]
    tm = TOKEN_TILE
    row = lambda w: pl.BlockSpec((tm, w), lambda i: (i, 0))
    return pl.pallas_call(
        _mixer_in_kernel,
        grid=(t // tm,),
        in_specs=[row(D_MODEL), _const_spec((1, D_MODEL)), _const_spec((D_MODEL, IN_COLS))],
        out_specs=[row(FNET_WIDTH), row(RWKV_IN), row(CONV_WIDTH), row(3 * D_MODEL)],
        out_shape=[jax.ShapeDtypeStruct((t, FNET_WIDTH), BF), jax.ShapeDtypeStruct((t, RWKV_IN), BF),
                   jax.ShapeDtypeStruct((t, CONV_WIDTH), F32), jax.ShapeDtypeStruct((t, 3 * D_MODEL), BF)],
        compiler_params=_cparams(("parallel",)),
        name="mixer_in",
    )(x, g.reshape(1, D_MODEL), w_in_bf)


def _fft_tables(seq):
    n2 = FFT_N2
    n1 = seq // n2
    k1 = np.arange(n1)[:, None].astype(np.float64)
    m1 = np.arange(n1)[None, :].astype(np.float64)
    ang1 = 2.0 * np.pi * ((k1 * m1) % n1) / n1
    f1 = np.concatenate([np.cos(ang1), -np.sin(ang1)], axis=0)
    m2 = np.arange(n2)[None, :].astype(np.float64)
    angt = 2.0 * np.pi * ((k1 * m2) % seq) / seq
    tr, ti = np.cos(angt), -np.sin(angt)
    k2 = np.arange(n2)[:, None].astype(np.float64)
    ang2 = 2.0 * np.pi * ((k2 * m2) % n2) / n2
    c2, s2 = np.cos(ang2), np.sin(ang2)
    f2 = np.block([[c2, s2], [-s2, c2]])
    q = np.arange(FNET_GROUP_DIM)
    angc = 2.0 * np.pi * ((q[:, None] * q[None, :]) % FNET_GROUP_DIM) / FNET_GROUP_DIM
    scale = 1.0 / math.sqrt(seq * FNET_GROUP_DIM)
    eye = np.eye(FNET_GROUPS)
    cd = np.concatenate([np.kron(eye, np.cos(angc)), np.kron(eye, np.sin(angc))], axis=0) * scale
    tr = jnp.repeat(jnp.asarray(tr, F32), FNET_WIDTH, axis=1)
    ti = jnp.repeat(jnp.asarray(ti, F32), FNET_WIDTH, axis=1)
    return (jnp.asarray(f1, BF), tr, ti, jnp.asarray(f2, BF), jnp.asarray(cd, BF))


def _fft1_kernel(x_ref, f1_ref, tr_ref, ti_ref, o_ref):
    n1 = x_ref.shape[1]
    res = jnp.dot(f1_ref[...], x_ref[0].astype(BF), preferred_element_type=F32)
    ar, ai = res[:n1], res[n1:]
    tr, ti = tr_ref[...], ti_ref[...]
    o_ref[0, 0] = ar * tr - ai * ti
    o_ref[0, 1] = ar * ti + ai * tr


def _fft2_kernel(z_ref, f2_ref, cd_ref, o_ref):
    n2 = FFT_N2
    kb = FFT_K1_BLOCK
    z = jnp.concatenate([jnp.concatenate([z_ref[0, 0, i], z_ref[0, 1, i]], axis=0) for i in range(kb)],
                        axis=1).astype(BF)
    g = jnp.dot(f2_ref[...], z, preferred_element_type=F32).astype(BF)
    gc = jnp.concatenate(
        [jnp.concatenate([g[:n2, i * FNET_WIDTH:(i + 1) * FNET_WIDTH], g[n2:, i * FNET_WIDTH:(i + 1) * FNET_WIDTH]],
                         axis=1) for i in range(kb)], axis=0)
    y = jnp.dot(gc, cd_ref[...], preferred_element_type=F32)
    for i in range(kb):
        o_ref[0, :, i * FNET_WIDTH:(i + 1) * FNET_WIDTH] = y[i * n2:(i + 1) * n2]


def fourier_mix(ua, bsz, seq, tables):
    f1, tr, ti, f2, cd = tables
    n2 = FFT_N2
    n1 = seq // n2
    cols = n2 * FNET_WIDTH
    cb = FFT_COLS
    x = ua.reshape(bsz, n1, cols)
    z = pl.pallas_call(
        _fft1_kernel,
        grid=(bsz, cols // cb),
        in_specs=[pl.BlockSpec((1, n1, cb), lambda b, j: (b, 0, j)),
                  _const_spec((2 * n1, n1)),
                  pl.BlockSpec((n1, cb), lambda b, j: (0, j)),
                  pl.BlockSpec((n1, cb), lambda b, j: (0, j))],
        out_specs=pl.BlockSpec((1, 2, n1, cb), lambda b, j: (b, 0, 0, j)),
        out_shape=jax.ShapeDtypeStruct((bsz, 2, n1, cols), F32),
        compiler_params=_cparams(("parallel", "parallel")),
        name="fft1",
    )(x, f1, tr, ti)
    z = z.reshape(bsz, 2, n1, n2, FNET_WIDTH)
    kb = FFT_K1_BLOCK
    y = pl.pallas_call(
        _fft2_kernel,
        grid=(bsz, n1 // kb),
        in_specs=[pl.BlockSpec((1, 2, kb, n2, FNET_WIDTH), lambda b, j: (b, 0, j, 0, 0)),
                  _const_spec((2 * n2, 2 * n2)),
                  _const_spec((2 * FNET_WIDTH, FNET_WIDTH))],
        out_specs=pl.BlockSpec((1, n2, kb * FNET_WIDTH), lambda b, j: (b, 0, j)),
        out_shape=jax.ShapeDtypeStruct((bsz, n2, n1 * FNET_WIDTH), F32),
        compiler_params=_cparams(("parallel", "parallel")),
        name="fft2",
    )(z, f2, cd)
    return y.reshape(bsz * seq, FNET_WIDTH)


PREP_HALO = 16


def _head_ones():
    h = np.arange(RWKV_WIDTH) // RWKV_HEAD_DIM
    return jnp.asarray((h[:, None] == h[None, :]).astype(np.float32), BF)


def _rwkv_prep_kernel(tiles_per_seq, u_ref, up_ref, un_ref, mup_ref, mun_ref, w0_ref, wup_ref, a0_ref, aup_ref,
                      gup_ref, kk_w_ref, ka_ref, rk_ref, ones_ref,
                      r_ref, kkn_ref, v_ref, krep_ref, kka_ref, logw_ref, bonus_ref, g_ref, buf):
    tm = u_ref.shape[0]
    i = pl.program_id(0)
    first = (i % tiles_per_seq) == 0
    last = (i % tiles_per_seq) == tiles_per_seq - 1
    buf[0:8] = jnp.where(first, 0.0, up_ref[...].astype(F32)[PREP_HALO - 8:])
    buf[8:tm + 8] = u_ref[...].astype(F32)
    buf[tm + 8:tm + 16] = jnp.where(last, 0.0, un_ref[...].astype(F32)[:8])
    mup, mun = mup_ref[...], mun_ref[...]
    mu0 = 1.0 - mup - mun

    def shifted(lo, hi):
        return (mu0[:, lo:hi] * buf[8:tm + 8, lo:hi] + mup[:, lo:hi] * buf[7:tm + 7, lo:hi]
                + mun[:, lo:hi] * buf[9:tm + 9, lo:hi])

    c0 = RWKV_WIDTH
    r = shifted(0, c0)
    k = shifted(c0, 2 * c0)
    v = shifted(2 * c0, 3 * c0)
    wd = shifted(3 * c0, 3 * c0 + 128)
    ad = shifted(3 * c0 + 128, 3 * c0 + 256)
    gd = shifted(3 * c0 + 256, 3 * c0 + 384)

    ones = ones_ref[...]
    r_ref[...] = r.astype(BF)
    v_ref[...] = v.astype(BF)
    kk = k * kk_w_ref[...]
    ss = _bdot(kk * kk, ones)
    kk = kk * lax.rsqrt(jnp.maximum(ss, 1e-24))
    kkn_ref[...] = kk.astype(BF)
    bonus_ref[...] = (_bdot(r * k * rk_ref[...], ones) * v).astype(BF)
    g_ref[...] = _bdot(_sigmoid(gd), gup_ref[...]).astype(BF)

    wlogit = _bdot(jnp.tanh(wd), wup_ref[...])
    alogit = _bdot(ad, aup_ref[...])
    ka = ka_ref[...]
    for d in range(N_DIR):
        sl = slice(d * c0, (d + 1) * c0)
        logw_ref[d] = -DECAY_SCALE * _sigmoid(w0_ref[:, sl] + wlogit[:, sl])
        a = _sigmoid(a0_ref[:, sl] + alogit[:, sl])
        krep_ref[d] = (k * (1.0 + (a - 1.0) * ka)).astype(BF)
        kka_ref[d] = (kk * a).astype(BF)


def _lora_block(w):
    z = jnp.zeros_like(w[0])
    return jnp.concatenate([jnp.concatenate([w[0], z], axis=1), jnp.concatenate([z, w[1]], axis=1)], axis=0)


def rwkv_prep(ub, seq, p, l):
    t = ub.shape[0]
    tm = TOKEN_TILE
    nbh = tm // PREP_HALO
    row = lambda w: pl.BlockSpec((tm, w), lambda i: (i, 0))
    row2 =pl.BlockSpec((N_DIR, tm, RWKV_WIDTH), lambda i: (0, i, 0))
    vec = lambda a: a.reshape(1, -1).astype(F32)
    sds = jax.ShapeDtypeStruct
    outs = pl.pallas_call(
        functools.partial(_rwkv_prep_kernel, seq // tm),
        grid=(t // tm,),
        in_specs=[row(RWKV_IN),
                  pl.BlockSpec((PREP_HALO, RWKV_IN), lambda i: (jnp.maximum(i * nbh - 1, 0), 0)),
                  pl.BlockSpec((PREP_HALO, RWKV_IN), lambda i: (jnp.minimum((i + 1) * nbh, t // PREP_HALO - 1), 0)),
                  _const_spec((1, RWKV_IN)), _const_spec((1, RWKV_IN)),
                  _const_spec((1, 2 * RWKV_WIDTH)), _const_spec((2 * W_LORA, 2 * RWKV_WIDTH)),
                  _const_spec((1, 2 * RWKV_WIDTH)), _const_spec((2 * A_LORA, 2 * RWKV_WIDTH)),
                  _const_spec((G_LORA, RWKV_WIDTH)),
                  _const_spec((1, RWKV_WIDTH)), _const_spec((1, RWKV_WIDTH)), _const_spec((1, RWKV_WIDTH)),
                  _const_spec((RWKV_WIDTH, RWKV_WIDTH))],
        out_specs=[row(RWKV_WIDTH), row(RWKV_WIDTH), row(RWKV_WIDTH), row2, row2, row2,
                   row(RWKV_WIDTH), row(RWKV_WIDTH)],
        out_shape=[sds((t, RWKV_WIDTH), BF)] * 3 + [sds((N_DIR, t, RWKV_WIDTH), BF)] * 2
                  + [sds((N_DIR, t, RWKV_WIDTH), F32)] + [sds((t, RWKV_WIDTH), BF)] * 2,
        scratch_shapes=[pltpu.VMEM((tm + 16, RWKV_IN), F32)],
        compiler_params=_cparams(("parallel",)),
        name="rwkv_prep",
    )(ub, ub, ub, vec(p['rwkv_mu_prev'][l]), vec(p['rwkv_mu_next'][l]),
      vec(p['rwkv_w0'][l]), _lora_block(p['rwkv_w_up'][l]).astype(BF),
      vec(p['rwkv_a0'][l]), _lora_block(p['rwkv_a_up'][l]).astype(BF),
      p['rwkv_g_up'][l].astype(BF), vec(p['rwkv_k_k'][l]), vec(p['rwkv_k_a'][l]), vec(p['rwkv_r_k'][l]),
      _head_ones())
    return outs


GROUP_HEADS = 4
GROUP = GROUP_HEADS * RWKV_HEAD_DIM
N_GROUP = RWKV_HEADS // GROUP_HEADS
SCAN_MAX_ITERS = 4
SCAN_BATCH = 8
SCAN_TOKENS = 1024


def _scan_kernel(r_ref, kk_ref, v_ref, krep_ref, kka_ref, logw_ref, o_ref, s_ref, *scratch):
    L = CHUNK
    nb = r_ref.shape[0]
    nch = r_ref.shape[1] // L
    n_iter = min(SCAN_MAX_ITERS, nch)
    cpi = nb * nch // n_iter
    steps_per_iter = nch // n_iter
    n_yield = 4 * steps_per_iter
    d = pl.program_id(1)
    step = pl.program_id(2)
    half = len(scratch) // 2
    sets = (scratch[:half], scratch[half:])

    @pl.when(step == 0)
    def _():
        s_ref[...] = jnp.zeros_like(s_ref)
        for ref in sets[1]:
            ref[...] = jnp.zeros_like(ref)

    sgn = 1 - 2 * d
    r64 = lax.broadcasted_iota(jnp.int32, (L, L), 0)
    c64 = lax.broadcasted_iota(jnp.int32, (L, L), 1)
    tri = jnp.where((r64 - c64) * sgn >= 0, 1.0, 0.0).astype(BF)
    row = lax.broadcasted_iota(jnp.int32, (L, GROUP), 0)
    sidx = lax.broadcasted_iota(jnp.int32, (L, GROUP), 1) % L
    diff = (row - sidx) * sgn
    strict = diff > 0
    incl = diff >= 0
    eye = jnp.where(row == sidx, 1.0, 0.0)
    brow = lax.broadcasted_iota(jnp.int32, (GROUP, GROUP), 0) // L
    bcol = lax.broadcasted_iota(jnp.int32, (GROUP, GROUP), 1) // L
    same_head = brow == bcol
    lane128 = lax.broadcasted_iota(jnp.int32, (L, 128), 1)
    half_ones = [jnp.where(lane128 < L, 1.0, 0.0).astype(BF), jnp.where(lane128 >= L, 1.0, 0.0).astype(BF)]
    zero_tile = jnp.zeros((L, 128), BF)

    def bdiag(x):
        xb = x.astype(BF)
        blocks = []
        for h in range(GROUP_HEADS):
            tile = xb[:, 128 * (h // 2):128 * (h // 2 + 1)] * half_ones[h % 2]
            blocks.append(jnp.concatenate([tile, zero_tile] if h < 2 else [zero_tile, tile], axis=1))
        return jnp.concatenate(blocks, axis=0)

    def off_mask(b):
        return ((row // (2 * b)) == (sidx // (2 * b))) & ((row // b) != (sidx // b))

    def stage2(it, src):
        ar_s, t_s, mrb_s, x0_s, o0_s, bk_s, v_s, dec_s = src
        ids = [(bi, gi) for bi in range(nb) for gi in range(N_GROUP)]
        gsl = [slice(gi * GROUP, (gi + 1) * GROUP) for gi in range(N_GROUP)]
        for q in range(steps_per_iter):
            jj = it * steps_per_iter + q
            j = jj + d * (nch - 1 - 2 * jj)
            sl = pl.ds(pl.multiple_of(j * L, L), L)
            s = {k: s_ref[k] for k in ids}
            arh = {k: _dot_t1(ar_s[k[0], j, k[1]], s[k].astype(BF)) for k in ids}
            yield
            u = {k: jnp.dot(t_s[k[0], j, k[1]], bdiag(arh[k][:L] + x0_s[k[0], j, k[1]]),
                            preferred_element_type=F32) for k in ids}
            yield
            upd = {}
            for bi, gi in ids:
                uv = jnp.concatenate([u[(bi, gi)].astype(BF), v_s[bi, j][:, gsl[gi]]], axis=0)
                upd[(bi, gi)] = _dot_t0(uv, bk_s[bi, j, gi])
            for bi, gi in ids:
                dec = dec_s[bi, j]
                s_ref[bi, gi] = s[(bi, gi)] * dec[0:1, gsl[gi]] + jnp.where(same_head, upd[(bi, gi)], 0.0)
            yield
            for bi, gi in ids:
                o = (arh[(bi, gi)][L:] + o0_s[bi, j, gi]
                     - jnp.dot(mrb_s[bi, j, gi], bdiag(u[(bi, gi)]), preferred_element_type=F32))
                o_ref[0, bi, sl, gsl[gi]] = o.astype(BF)
            yield

    def body(it, dst, src):
        ar_s, t_s, mrb_s, x0_s, o0_s, bk_s, v_s, dec_s = dst
        other = stage2(it, src)
        n_slots = 14

        def tick(slot):
            for _ in range((n_yield * (slot + 1)) // n_slots - (n_yield * slot) // n_slots):
                next(other, None)

        chains = []
        for q in range(cpi):
            cid = it * cpi + q
            bi = cid // nch
            j = cid % nch
            sl = pl.ds(pl.multiple_of(j * L, L), L)
            lw = logw_ref[0, bi, sl, :]
            lwh = lw.astype(BF)
            lwl = (lw - lwh.astype(F32)).astype(BF)
            g = jnp.dot(tri, lwh, preferred_element_type=F32) + jnp.dot(tri, lwl, preferred_element_type=F32)
            gtot = jnp.sum(lw, axis=0, keepdims=True)
            e_e = jnp.exp(g - lw)
            e_mg = jnp.exp(-g)
            e_gt = jnp.exp(gtot - g)
            kk = kk_ref[bi, sl, :].astype(F32)
            kka = kka_ref[0, bi, sl, :].astype(F32)
            krep = krep_ref[0, bi, sl, :].astype(F32)
            at = kk * e_e
            bh = kka * e_mg
            kh = krep * e_mg
            rh = r_ref[bi, sl, :].astype(F32) * jnp.exp(g)
            bg = kka * e_gt
            kg = krep * e_gt
            vv = v_ref[bi, sl, :]
            dec_s[bi, j] = jnp.broadcast_to(jnp.exp(gtot), (8, RWKV_WIDTH))
            v_s[bi, j] = vv
            for gi in range(N_GROUP):
                gs = slice(gi * GROUP, (gi + 1) * GROUP)
                ar = jnp.concatenate([at[:, gs], rh[:, gs]], axis=0).astype(BF)
                ar_s[bi, j, gi] = ar
                bk_s[bi, j, gi] = jnp.concatenate([-bg[:, gs], kg[:, gs]], axis=0).astype(BF)
                chains.append(dict(ix=(bi, j, gi), ar=ar, bh=bh[:, gs], kh=kh[:, gs], v=vv[:, gs]))
        tick(0)
        for ch in chains:
            ch['nb'] = _dot_t1(ch['ar'], bdiag(ch['bh']))
        tick(1)
        for ch in chains:
            ch['nk'] = _dot_t1(ch['ar'], bdiag(ch['kh']))
        tick(2)
        for ch in chains:
            nk = ch['nk']
            lhs = jnp.concatenate([jnp.where(strict, nk[:L], 0.0), jnp.where(incl, nk[L:], 0.0)], axis=0)
            xo = jnp.dot(lhs.astype(BF), bdiag(ch['v']), preferred_element_type=F32)
            x0_s[ch['ix']] = xo[:L]
            o0_s[ch['ix']] = xo[L:]
        tick(3)
        for ch in chains:
            nbm = ch['nb']
            mrb_s[ch['ix']] = jnp.where(incl, nbm[L:], 0.0).astype(BF)
            ch['n'] = jnp.where(strict, nbm[:L], 0.0)
            ch['t'] = eye - jnp.where(off_mask(1), ch['n'], 0.0)
        slot = 4
        b = 2
        while b < L:
            om = off_mask(b)
            for ch in chains:
                ch['p'] = jnp.dot(ch['t'].astype(BF), bdiag(jnp.where(om, ch['n'], 0.0)),
                                  preferred_element_type=F32)
            tick(slot)
            for ch in chains:
                ch['t'] = ch['t'] - jnp.dot(ch['p'].astype(BF), bdiag(ch['t']), preferred_element_type=F32)
            tick(slot + 1)
            slot += 2
            b *= 2
        assert slot == n_slots
        for ch in chains:
            t_s[ch['ix']] = ch['t'].astype(BF)
        for _ in other:
            pass

    for par in range(2):
        @pl.when(step % 2 == par)
        def _(par=par):
            def loop_body(it, carry):
                body(it, sets[par], sets[1 - par])
                return carry
            lax.fori_loop(0, n_iter, loop_body, 0)


def rwkv_scan(r, kk, v, krep, kka, logw, bsz, seq):
    t = r.shape[0]
    nb = min(SCAN_BATCH, bsz)
    ts = SCAN_TOKENS // nb
    nc = seq // ts
    nch = ts // CHUNK

    def tile_in(d, c):
        c = jnp.minimum(c, nc - 1)
        return c + d * (nc - 1 - 2 * c)

    def tile_out(d, c):
        c = jnp.maximum(c - 1, 0)
        return c + d * (nc - 1 - 2 * c)

    shared = pl.BlockSpec((nb, ts, RWKV_WIDTH), lambda b, d, c: (b, tile_in(d, c), 0))
    perdir = pl.BlockSpec((1, nb, ts, RWKV_WIDTH), lambda b, d, c: (d, b, tile_in(d, c), 0))
    outspec = pl.BlockSpec((1, nb, ts, RWKV_WIDTH), lambda b, d, c: (d, b, tile_out(d, c), 0))
    shape3 = (bsz, seq, RWKV_WIDTH)
    shape4 = (N_DIR, bsz, seq, RWKV_WIDTH)
    per = (nb, nch, N_GROUP)
    factor_set = [pltpu.VMEM(per + (2 * CHUNK, GROUP), BF),
                  pltpu.VMEM(per + (CHUNK, GROUP), BF),
                  pltpu.VMEM(per + (CHUNK, GROUP), BF),
                  pltpu.VMEM(per + (CHUNK, GROUP), F32),
                  pltpu.VMEM(per + (CHUNK, GROUP), F32),
                  pltpu.VMEM(per + (2 * CHUNK, GROUP), BF),
                  pltpu.VMEM((nb, nch, CHUNK, RWKV_WIDTH), BF),
                  pltpu.VMEM((nb, nch, 8, RWKV_WIDTH), F32)]
    o = pl.pallas_call(
        _scan_kernel,
        grid=(bsz // nb, N_DIR, nc + 1),
        in_specs=[shared, shared, shared, perdir, perdir, perdir],
        out_specs=outspec,
        out_shape=jax.ShapeDtypeStruct(shape4, BF),
        scratch_shapes=[pltpu.VMEM((nb, N_GROUP, GROUP, GROUP), F32)] + factor_set + factor_set,
        compiler_params=_cparams(("parallel", "arbitrary", "arbitrary")),
        name="rwkv_scan",
    )(r.reshape(shape3), kk.reshape(shape3), v.reshape(shape3), krep.reshape(shape4), kka.reshape(shape4),
      logw.reshape(shape4))
    return o.reshape(N_DIR, t, RWKV_WIDTH)


CONV_HALO = 16


def _conv_kernel(tiles_per_seq, h_ref, hp_ref, hn_ref, w_ref, b_ref, lnw_ref, lnb_ref, o_ref, buf, part):
    tm = h_ref.shape[0]
    i = pl.program_id(0)
    first = (i % tiles_per_seq) == 0
    last = (i % tiles_per_seq) == tiles_per_seq - 1
    buf[0:CONV_HALO] = jnp.where(first, 0.0, hp_ref[...])
    buf[CONV_HALO:tm + CONV_HALO] = h_ref[...]
    buf[tm + CONV_HALO:tm + 2 * CONV_HALO] = jnp.where(last, 0.0, hn_ref[...])
    base = CONV_HALO - CONV_KERNEL // 2
    acc = jnp.zeros((tm, CONV_WIDTH), F32) + b_ref[...]
    for b in range(8):
        pb = None
        for a in range((base + CONV_KERNEL - 1) // 8 + 1):
            k = 8 * a + b - base
            if 0 <= k < CONV_KERNEL:
                term = w_ref[k:k + 1, :] * buf[8 * a:8 * a + tm + 8, :]
                pb = term if pb is None else pb + term
        part[...] = pb
        acc = acc + part[b:b + tm, :]
    mu = jnp.mean(acc, axis=-1, keepdims=True)
    xc = acc - mu
    var = jnp.mean(xc * xc, axis=-1, keepdims=True)
    y = xc * lax.rsqrt(var + LN_EPS) * lnw_ref[...] + lnb_ref[...]
    o_ref[...] = y * _sigmoid(y)


def conformer_conv(hc, seq, p, l):
    t = hc.shape[0]
    tm = TOKEN_TILE
    nbh = tm // CONV_HALO
    row = pl.BlockSpec((tm, CONV_WIDTH), lambda i: (i, 0))
    vec = lambda a: a.reshape(1, -1).astype(F32)
    return pl.pallas_call(
        functools.partial(_conv_kernel, seq // tm),
        grid=(t // tm,),
        in_specs=[row,
                  pl.BlockSpec((CONV_HALO, CONV_WIDTH), lambda i: (jnp.maximum(i * nbh - 1, 0), 0)),
                  pl.BlockSpec((CONV_HALO, CONV_WIDTH),
                               lambda i: (jnp.minimum((i + 1) * nbh, t // CONV_HALO - 1), 0)),
                  _const_spec((CONV_KERNEL, CONV_WIDTH)), _const_spec((1, CONV_WIDTH)),
                  _const_spec((1, CONV_WIDTH)), _const_spec((1, CONV_WIDTH))],
        out_specs=row,
        out_shape=jax.ShapeDtypeStruct((t, CONV_WIDTH), F32),
        scratch_shapes=[pltpu.VMEM((tm + 2 * CONV_HALO, CONV_WIDTH), F32),
                        pltpu.VMEM((tm + 8, CONV_WIDTH), F32)],
        compiler_params=_cparams(("parallel",)),
        name="conformer_conv",
    )(hc, hc, hc, p['conv_dw_w'][l], vec(p['conv_dw_b'][l]), vec(p['conv_ln_w'][l]), vec(p['conv_ln_b'][l]))


def _merge_kernel(x_ref, ya_ref, o_ref, bonus_ref, g_ref, yc_ref, gate_ref, ones_ref, gnw_ref, gnb_ref,
                  wa_ref, wb_ref, wc_ref, wo_ref, out_ref):
    ones = ones_ref[...]
    o = o_ref[0].astype(F32) + o_ref[1].astype(F32)
    inv = 1.0 / RWKV_HEAD_DIM
    mu = _bdot(o, ones) * inv
    oc = o - mu
    var = _bdot(oc * oc, ones) * inv
    on = oc * lax.rsqrt(var + GN_EPS) * gnw_ref[...] + gnb_ref[...]
    yb = (on + bonus_ref[...]) * g_ref[...]
    m = gate_ref[:, 0:D_MODEL].astype(F32) * _bdot(ya_ref[...], wa_ref[...])
    m = m + gate_ref[:, D_MODEL:2 * D_MODEL].astype(F32) * _bdot(yb, wb_ref[...])
    m = m + gate_ref[:, 2 * D_MODEL:3 * D_MODEL].astype(F32) * _bdot(yc_ref[...], wc_ref[...])
    out_ref[...] = x_ref[...] + _bdot(m, wo_ref[...])


def merge(x, ya, o, bonus, g, yc, gates, p, l):
    t = x.shape[0]
    tm = TOKEN_TILE
    row = lambda w: pl.BlockSpec((tm, w), lambda i: (i, 0))
    vec = lambda a: a.reshape(1, -1).astype(F32)
    return pl.pallas_call(
        _merge_kernel,
        grid=(t // tm,),
        in_specs=[row(D_MODEL), row(FNET_WIDTH),
                  pl.BlockSpec((N_DIR, tm, RWKV_WIDTH), lambda i: (0, i, 0)),
                  row(RWKV_WIDTH), row(RWKV_WIDTH), row(CONV_WIDTH), row(3 * D_MODEL),
                  _const_spec((RWKV_WIDTH, RWKV_WIDTH)), _const_spec((1, RWKV_WIDTH)), _const_spec((1, RWKV_WIDTH)),
                  _const_spec((FNET_WIDTH, D_MODEL)), _const_spec((RWKV_WIDTH, D_MODEL)),
                  _const_spec((CONV_WIDTH, D_MODEL)), _const_spec((D_MODEL, D_MODEL))],
        out_specs=row(D_MODEL),
        out_shape=jax.ShapeDtypeStruct((t, D_MODEL), F32),
        compiler_params=_cparams(("parallel",)),
        name="merge",
    )(x, ya, o, bonus, g, yc, gates, _head_ones(), vec(p['rwkv_gn_w'][l]), vec(p['rwkv_gn_b'][l]),
      p['fnet_w'][l].astype(BF), p['rwkv_w_o'][l].astype(BF), p['conv_w_o'][l].astype(BF),
      p['mix_w_out'][l].astype(BF))


FFN_HALO = 16


def _erf(x):
    return lax.erf(x)


def _ffn_kernel(tiles_per_seq, final, x_ref, xp_ref, xn_ref, g_ref, wup_ref, dww_ref, dwb_ref, wdn_ref, gf_ref,
                out_ref, xn_s, h_s):
    tm = x_ref.shape[0]
    i = pl.program_id(0)
    first = (i % tiles_per_seq) == 0
    last = (i % tiles_per_seq) == tiles_per_seq - 1
    g = g_ref[...]
    hl = FFN_HALO
    xn_s[0:hl] = _rms(xp_ref[...], g).astype(BF)
    xn_s[hl:tm + hl] = _rms(x_ref[...], g).astype(BF)
    xn_s[tm + hl:tm + 2 * hl] = _rms(xn_ref[...], g).astype(BF)
    rows = lax.broadcasted_iota(jnp.int32, (tm + 2 * hl, 1), 0)
    pad = (first & (rows < hl)) | (last & (rows >= tm + hl))
    acc = jnp.zeros((tm, D_MODEL), F32)
    for lo, hi in FF_BLOCKS:
        h = jnp.dot(xn_s[...], wup_ref[:, lo:hi], preferred_element_type=F32)
        h_s[:, 0:hi - lo] = jnp.where(pad, 0.0, h)
        hc = (dww_ref[0:1, lo:hi] * h_s[hl - 1:tm + hl - 1, 0:hi - lo]
              + dww_ref[1:2, lo:hi] * h_s[hl:tm + hl, 0:hi - lo]
              + dww_ref[2:3, lo:hi] * h_s[hl + 1:tm + hl + 1, 0:hi - lo] + dwb_ref[:, lo:hi])
        gate = jnp.dot(xn_s[hl:tm + hl], wup_ref[:, D_FF + lo:D_FF + hi], preferred_element_type=F32)
        act = 0.5 * hc * (1.0 + _erf(hc * (1.0 / math.sqrt(2.0)))) * gate
        acc = acc + jnp.dot(act.astype(BF), wdn_ref[lo:hi, :], preferred_element_type=F32)
    y = x_ref[...] + acc
    if final:
        y = _rms(y, gf_ref[...])
    out_ref[...] = y


def ffn(x, seq, p, l, final):
    t = x.shape[0]
    tm = TOKEN_TILE
    hl = FFN_HALO
    nbh = tm // hl
    row = pl.BlockSpec((tm, D_MODEL), lambda i: (i, 0))
    vec = lambda a: a.reshape(1, -1).astype(F32)
    fb = max(hi - lo for lo, hi in FF_BLOCKS)
    return pl.pallas_call(
        functools.partial(_ffn_kernel, seq // tm, final),
        grid=(t // tm,),
        in_specs=[row,
                  pl.BlockSpec((hl, D_MODEL), lambda i: (jnp.maximum(i * nbh - 1, 0), 0)),
                  pl.BlockSpec((hl, D_MODEL), lambda i: (jnp.minimum((i + 1) * nbh, t // hl - 1), 0)),
                  _const_spec((1, D_MODEL)), _const_spec((D_MODEL, 2 * D_FF)),
                  _const_spec((3, D_FF)), _const_spec((1, D_FF)), _const_spec((D_FF, D_MODEL)),
                  _const_spec((1, D_MODEL))],
        out_specs=row,
        out_shape=jax.ShapeDtypeStruct((t, D_MODEL), F32),
        scratch_shapes=[pltpu.VMEM((tm + 2 * hl, D_MODEL), BF), pltpu.VMEM((tm + 2 * hl, fb), F32)],
        compiler_params=_cparams(("parallel",)),
        name="ffn",
    )(x, x, x, vec(p['ffn_norm_g'][l]), p['ffn_w_up'][l].astype(BF), p['ffn_dw_w'][l],
      vec(p['ffn_dw_b'][l]), p['ffn_w_down'][l].astype(BF), vec(p['final_norm_g']))


def encoder(x3, p):
    bsz, seq, _ = x3.shape
    depth = p['w_in'].shape[0]
    x = x3.reshape(bsz * seq, D_MODEL)
    tables = _fft_tables(seq)
    for l in range(depth):
        ua, ub, hc, gates = mixer_in(x, p['attn_norm_g'][l], p['w_in'][l].astype(BF))
        ya = fourier_mix(ua, bsz, seq, tables)
        r, kk, v, krep, kka, logw, bonus, g = rwkv_prep(ub, seq, p, l)
        o = rwkv_scan(r, kk, v, krep, kka, logw, bsz, seq)
        yc = conformer_conv(hc, seq, p, l)
        x = merge(x, ya, o, bonus, g, yc, gates, p, l)
        x = ffn(x, seq, p, l, final=(l == depth - 1))
    return x.reshape(bsz, seq, D_MODEL)


def kernel(x_prompt, x_sample, attn_norm_g, w_in, fnet_w, rwkv_mu_prev, rwkv_mu_next, rwkv_w0, rwkv_w_up,
           rwkv_a0, rwkv_a_up, rwkv_g_up, rwkv_k_k, rwkv_k_a, rwkv_r_k, rwkv_gn_w, rwkv_gn_b, rwkv_w_o,
           conv_dw_w, conv_dw_b, conv_ln_w, conv_ln_b, conv_w_o, mix_w_out, ffn_norm_g, ffn_w_up,
           ffn_dw_w, ffn_dw_b, ffn_w_down, final_norm_g):
    p = dict(attn_norm_g=attn_norm_g, w_in=w_in, fnet_w=fnet_w, rwkv_mu_prev=rwkv_mu_prev,
             rwkv_mu_next=rwkv_mu_next, rwkv_w0=rwkv_w0, rwkv_w_up=rwkv_w_up, rwkv_a0=rwkv_a0,
             rwkv_a_up=rwkv_a_up, rwkv_g_up=rwkv_g_up, rwkv_k_k=rwkv_k_k, rwkv_k_a=rwkv_k_a, rwkv_r_k=rwkv_r_k,
             rwkv_gn_w=rwkv_gn_w, rwkv_gn_b=rwkv_gn_b, rwkv_w_o=rwkv_w_o, conv_dw_w=conv_dw_w,
             conv_dw_b=conv_dw_b, conv_ln_w=conv_ln_w, conv_ln_b=conv_ln_b, conv_w_o=conv_w_o,
             mix_w_out=mix_w_out, ffn_norm_g=ffn_norm_g, ffn_w_up=ffn_w_up, ffn_dw_w=ffn_dw_w,
             ffn_dw_b=ffn_dw_b, ffn_w_down=ffn_w_down, final_norm_g=final_norm_g)
    return (encoder(x_prompt, p), encoder(x_sample, p))
```

```python
import functools
import math

import numpy as np
import jax
import jax.numpy as jnp
from jax import lax
from jax.experimental import pallas as pl
from jax.experimental.pallas import tpu as pltpu

D_MODEL = 1024
FNET_GROUPS = 4
FNET_GROUP_DIM = 64
FNET_WIDTH = 256
RWKV_HEADS = 8
RWKV_HEAD_DIM = 64
RWKV_WIDTH = 512
N_DIR = 2
W_LORA = 64
A_LORA = 64
G_LORA = 128
RWKV_IN = 1920
CONV_WIDTH = 256
CONV_KERNEL = 31
IN_COLS = 5760
D_FF = 2816
RMS_EPS = 1e-6
LN_EPS = 1e-5
GN_EPS = 64e-5
DECAY_SCALE = math.exp(-0.5)

COL_A = 0
COL_B = FNET_WIDTH
COL_C = COL_B + RWKV_IN
COL_G = COL_C + 2 * CONV_WIDTH

BF = jnp.bfloat16
F32 = jnp.float32

TOKEN_TILE = 512
CHUNK = 64
FFT_N2 = 64
FFT_COLS = 2048
FFT_K1_BLOCK = 8
VMEM_LIMIT = 56 * 1024 * 1024
FF_BLOCKS = ((0, 1024), (1024, 2048), (2048, 2816))


def _cparams(sem):
    return pltpu.CompilerParams(dimension_semantics=sem, vmem_limit_bytes=VMEM_LIMIT)


def _bdot(a, b):
    return jnp.dot(a.astype(BF), b.astype(BF), preferred_element_type=F32)


def _dot_t0(a, b):
    return lax.dot_general(a, b, (((0,), (0,)), ((), ())), preferred_element_type=F32)


def _dot_t1(a, b):
    return lax.dot_general(a, b, (((1,), (1,)), ((), ())), preferred_element_type=F32)


def _sigmoid(x):
    return 0.5 * jnp.tanh(0.5 * x) + 0.5


def _rms(x, g):
    return x * lax.rsqrt(jnp.mean(x * x, axis=-1, keepdims=True) + RMS_EPS) * g


def _const_spec(shape):
    nd = len(shape)
    return pl.BlockSpec(shape, lambda *_: (0,) * nd)


def _mixer_in_kernel(x_ref, g_ref, w_ref, ua_ref, ub_ref, hc_ref, gate_ref):
    xn = _rms(x_ref[...], g_ref[...]).astype(BF)
    ua_ref[...] = jnp.dot(xn, w_ref[:, COL_A:COL_B], preferred_element_type=F32).astype(BF)
    for lo, hi in ((0, 1024), (1024, RWKV_IN)):
        ub_ref[:, lo:hi] = jnp.dot(xn, w_ref[:, COL_B + lo:COL_B + hi], preferred_element_type=F32).astype(BF)
    uc = jnp.dot(xn, w_ref[:, COL_C:COL_G], preferred_element_type=F32)
    hc_ref[...] = (uc[:, :CONV_WIDTH] * _sigmoid(uc[:, CONV_WIDTH:])).astype(BF)
    for j in range(3):
        lo = j * D_MODEL
        ug = jnp.dot(xn, w_ref[:, COL_G + lo:COL_G + lo + D_MODEL], preferred_element_type=F32)
        gate_ref[:, lo:lo + D_MODEL] = _sigmoid(ug).astype(BF)


def mixer_in(x, g, w_in_bf):
    t = x.shape[0]
    tm = TOKEN_TILE
    row = lambda w: pl.BlockSpec((tm, w), lambda i: (i, 0))
    return pl.pallas_call(
        _mixer_in_kernel,
        grid=(t // tm,),
        in_specs=[row(D_MODEL), _const_spec((1, D_MODEL)), _const_spec((D_MODEL, IN_COLS))],
        out_specs=[row(FNET_WIDTH), row(RWKV_IN), row(CONV_WIDTH), row(3 * D_MODEL)],
        out_shape=[jax.ShapeDtypeStruct((t, FNET_WIDTH), BF), jax.ShapeDtypeStruct((t, RWKV_IN), BF),
                   jax.ShapeDtypeStruct((t, CONV_WIDTH), BF), jax.ShapeDtypeStruct((t, 3 * D_MODEL), BF)],
        compiler_params=_cparams(("parallel",)),
        name="mixer_in",
    )(x, g.reshape(1, D_MODEL), w_in_bf)


def _fft_tables(seq):
    n2 = FFT_N2
    n1 = seq // n2
    k1 = np.arange(n1)[:, None].astype(np.float64)
    m1 = np.arange(n1)[None, :].astype(np.float64)
    ang1 = 2.0 * np.pi * ((k1 * m1) % n1) / n1
    f1 = np.concatenate([np.cos(ang1), -np.sin(ang1)], axis=0)
    m2 = np.arange(n2)[None, :].astype(np.float64)
    angt = 2.0 * np.pi * ((k1 * m2) % seq) / seq
    tr, ti = np.cos(angt), -np.sin(angt)
    k2 = np.arange(n2)[:, None].astype(np.float64)
    ang2 = 2.0 * np.pi * ((k2 * m2) % n2) / n2
    c2, s2 = np.cos(ang2), np.sin(ang2)
    f2 = np.block([[c2, s2], [-s2, c2]])
    q = np.arange(FNET_GROUP_DIM)
    angc = 2.0 * np.pi * ((q[:, None] * q[None, :]) % FNET_GROUP_DIM) / FNET_GROUP_DIM
    scale = 1.0 / math.sqrt(seq * FNET_GROUP_DIM)
    eye = np.eye(FNET_GROUPS)
    cd = np.concatenate([np.kron(eye, np.cos(angc)), np.kron(eye, np.sin(angc))], axis=0) * scale
    tr = jnp.repeat(jnp.asarray(tr, F32), FNET_WIDTH, axis=1)
    ti = jnp.repeat(jnp.asarray(ti, F32), FNET_WIDTH, axis=1)
    return (jnp.asarray(f1, BF), tr, ti, jnp.asarray(f2, BF), jnp.asarray(cd, BF))


def _fft1_kernel(x_ref, f1_ref, tr_ref, ti_ref, o_ref):
    n1 = x_ref.shape[1]
    res = jnp.dot(f1_ref[...], x_ref[0].astype(BF), preferred_element_type=F32)
    ar, ai = res[:n1], res[n1:]
    tr, ti = tr_ref[...], ti_ref[...]
    o_ref[0, 0] = (ar * tr - ai * ti).astype(BF)
    o_ref[0, 1] = (ar * ti + ai * tr).astype(BF)


def _fft2_kernel(z_ref, f2_ref, cd_ref, o_ref):
    n2 = FFT_N2
    kb = FFT_K1_BLOCK
    z = jnp.concatenate([jnp.concatenate([z_ref[0, 0, i], z_ref[0, 1, i]], axis=0) for i in range(kb)],
                        axis=1).astype(BF)
    g = jnp.dot(f2_ref[...], z, preferred_element_type=F32).astype(BF)
    gc = jnp.concatenate(
        [jnp.concatenate([g[:n2, i * FNET_WIDTH:(i + 1) * FNET_WIDTH], g[n2:, i * FNET_WIDTH:(i + 1) * FNET_WIDTH]],
                         axis=1) for i in range(kb)], axis=0)
    y = jnp.dot(gc, cd_ref[...], preferred_element_type=F32)
    for i in range(kb):
        o_ref[0, :, i * FNET_WIDTH:(i + 1) * FNET_WIDTH] = y[i * n2:(i + 1) * n2].astype(BF)


def fourier_mix(ua, bsz, seq, tables):
    f1, tr, ti, f2, cd = tables
    n2 = FFT_N2
    n1 = seq // n2
    cols = n2 * FNET_WIDTH
    cb = FFT_COLS
    x = ua.reshape(bsz, n1, cols)
    z = pl.pallas_call(
        _fft1_kernel,
        grid=(bsz, cols // cb),
        in_specs=[pl.BlockSpec((1, n1, cb), lambda b, j: (b, 0, j)),
                  _const_spec((2 * n1, n1)),
                  pl.BlockSpec((n1, cb), lambda b, j: (0, j)),
                  pl.BlockSpec((n1, cb), lambda b, j: (0, j))],
        out_specs=pl.BlockSpec((1, 2, n1, cb), lambda b, j: (b, 0, 0, j)),
        out_shape=jax.ShapeDtypeStruct((bsz, 2, n1, cols), BF),
        compiler_params=_cparams(("parallel", "parallel")),
        name="fft1",
    )(x, f1, tr, ti)
    z = z.reshape(bsz, 2, n1, n2, FNET_WIDTH)
    kb = FFT_K1_BLOCK
    y = pl.pallas_call(
        _fft2_kernel,
        grid=(bsz, n1 // kb),
        in_specs=[pl.BlockSpec((1, 2, kb, n2, FNET_WIDTH), lambda b, j: (b, 0, j, 0, 0)),
                  _const_spec((2 * n2, 2 * n2)),
                  _const_spec((2 * FNET_WIDTH, FNET_WIDTH))],
        out_specs=pl.BlockSpec((1, n2, kb * FNET_WIDTH), lambda b, j: (b, 0, j)),
        out_shape=jax.ShapeDtypeStruct((bsz, n2, n1 * FNET_WIDTH), BF),
        compiler_params=_cparams(("parallel", "parallel")),
        name="fft2",
    )(z, f2, cd)
    return y.reshape(bsz * seq, FNET_WIDTH)


PREP_HALO = 16


def _head_ones():
    h = np.arange(RWKV_WIDTH) // RWKV_HEAD_DIM
    return jnp.asarray((h[:, None] == h[None, :]).astype(np.float32), BF)


def _rwkv_prep_kernel(tiles_per_seq, u_ref, up_ref, un_ref, mup_ref, mun_ref, w0_ref, wup_ref, a0_ref, aup_ref,
                      gup_ref, kk_w_ref, ka_ref, rk_ref, ones_ref,
                      r_ref, kkn_ref, v_ref, krep_ref, kka_ref, logw_ref, bonus_ref, g_ref, buf):
    tm = u_ref.shape[0]
    i = pl.program_id(0)
    first = (i % tiles_per_seq) == 0
    last = (i % tiles_per_seq) == tiles_per_seq - 1
    buf[0:8] = jnp.where(first, 0.0, up_ref[...].astype(F32)[PREP_HALO - 8:])
    buf[8:tm + 8] = u_ref[...].astype(F32)
    buf[tm + 8:tm + 16] = jnp.where(last, 0.0, un_ref[...].astype(F32)[:8])
    mup, mun = mup_ref[...], mun_ref[...]
    mu0 = 1.0 - mup - mun

    def shifted(lo, hi):
        return (mu0[:, lo:hi] * buf[8:tm + 8, lo:hi] + mup[:, lo:hi] * buf[7:tm + 7, lo:hi]
                + mun[:, lo:hi] * buf[9:tm + 9, lo:hi])

    c0 = RWKV_WIDTH
    r = shifted(0, c0)
    k = shifted(c0, 2 * c0)
    v = shifted(2 * c0, 3 * c0)
    wd = shifted(3 * c0, 3 * c0 + 128)
    ad = shifted(3 * c0 + 128, 3 * c0 + 256)
    gd = shifted(3 * c0 + 256, 3 * c0 + 384)

    ones = ones_ref[...]
    r_ref[...] = r.astype(BF)
    v_ref[...] = v.astype(BF)
    kk = k * kk_w_ref[...]
    ss = _bdot(kk * kk, ones)
    kk = kk * lax.rsqrt(jnp.maximum(ss, 1e-24))
    kkn_ref[...] = kk.astype(BF)
    bonus_ref[...] = (_bdot(r * k * rk_ref[...], ones) * v).astype(BF)
    g_ref[...] = _bdot(_sigmoid(gd), gup_ref[...]).astype(BF)

    wlogit = _bdot(jnp.tanh(wd), wup_ref[...])
    alogit = _bdot(ad, aup_ref[...])
    ka = ka_ref[...]
    for d in range(N_DIR):
        sl = slice(d * c0, (d + 1) * c0)
        logw_ref[d] = -DECAY_SCALE * _sigmoid(w0_ref[:, sl] + wlogit[:, sl])
        a = _sigmoid(a0_ref[:, sl] + alogit[:, sl])
        krep_ref[d] = (k * (1.0 + (a - 1.0) * ka)).astype(BF)
        kka_ref[d] = (kk * a).astype(BF)


def _lora_block(w):
    z = jnp.zeros_like(w[0])
    return jnp.concatenate([jnp.concatenate([w[0], z], axis=1), jnp.concatenate([z, w[1]], axis=1)], axis=0)


def rwkv_prep(ub, seq, p, l):
    t = ub.shape[0]
    tm = TOKEN_TILE
    nbh = tm // PREP_HALO
    row = lambda w: pl.BlockSpec((tm, w), lambda i: (i, 0))
    row2 =pl.BlockSpec((N_DIR, tm, RWKV_WIDTH), lambda i: (0, i, 0))
    vec = lambda a: a.reshape(1, -1).astype(F32)
    sds = jax.ShapeDtypeStruct
    outs = pl.pallas_call(
        functools.partial(_rwkv_prep_kernel, seq // tm),
        grid=(t // tm,),
        in_specs=[row(RWKV_IN),
                  pl.BlockSpec((PREP_HALO, RWKV_IN), lambda i: (jnp.maximum(i * nbh - 1, 0), 0)),
                  pl.BlockSpec((PREP_HALO, RWKV_IN), lambda i: (jnp.minimum((i + 1) * nbh, t // PREP_HALO - 1), 0)),
                  _const_spec((1, RWKV_IN)), _const_spec((1, RWKV_IN)),
                  _const_spec((1, 2 * RWKV_WIDTH)), _const_spec((2 * W_LORA, 2 * RWKV_WIDTH)),
                  _const_spec((1, 2 * RWKV_WIDTH)), _const_spec((2 * A_LORA, 2 * RWKV_WIDTH)),
                  _const_spec((G_LORA, RWKV_WIDTH)),
                  _const_spec((1, RWKV_WIDTH)), _const_spec((1, RWKV_WIDTH)), _const_spec((1, RWKV_WIDTH)),
                  _const_spec((RWKV_WIDTH, RWKV_WIDTH))],
        out_specs=[row(RWKV_WIDTH), row(RWKV_WIDTH), row(RWKV_WIDTH), row2, row2, row2,
                   row(RWKV_WIDTH), row(RWKV_WIDTH)],
        out_shape=[sds((t, RWKV_WIDTH), BF)] * 3 + [sds((N_DIR, t, RWKV_WIDTH), BF)] * 2
                  + [sds((N_DIR, t, RWKV_WIDTH), F32)] + [sds((t, RWKV_WIDTH), BF)] * 2,
        scratch_shapes=[pltpu.VMEM((tm + 16, RWKV_IN), F32)],
        compiler_params=_cparams(("parallel",)),
        name="rwkv_prep",
    )(ub, ub, ub, vec(p['rwkv_mu_prev'][l]), vec(p['rwkv_mu_next'][l]),
      vec(p['rwkv_w0'][l]), _lora_block(p['rwkv_w_up'][l]).astype(BF),
      vec(p['rwkv_a0'][l]), _lora_block(p['rwkv_a_up'][l]).astype(BF),
      p['rwkv_g_up'][l].astype(BF), vec(p['rwkv_k_k'][l]), vec(p['rwkv_k_a'][l]), vec(p['rwkv_r_k'][l]),
      _head_ones())
    return outs


GROUP_HEADS = 4
GROUP = GROUP_HEADS * RWKV_HEAD_DIM
N_GROUP = RWKV_HEADS // GROUP_HEADS
SCAN_MAX_ITERS = 4
SCAN_BATCH = 8
SCAN_TOKENS = 1024


def _scan_kernel(r_ref, kk_ref, v_ref, krep_ref, kka_ref, logw_ref, o_ref, s_ref, *scratch):
    L = CHUNK
    nb = r_ref.shape[0]
    nch = r_ref.shape[1] // L
    n_iter = min(SCAN_MAX_ITERS, nch)
    cpi = nb * nch // n_iter
    steps_per_iter = nch // n_iter
    n_yield = 4 * steps_per_iter
    d = pl.program_id(1)
    step = pl.program_id(2)
    half = len(scratch) // 2
    sets = (scratch[:half], scratch[half:])

    @pl.when(step == 0)
    def _():
        s_ref[...] = jnp.zeros_like(s_ref)
        for ref in sets[1]:
            ref[...] = jnp.zeros_like(ref)

    sgn = 1 - 2 * d
    r64 = lax.broadcasted_iota(jnp.int32, (L, L), 0)
    c64 = lax.broadcasted_iota(jnp.int32, (L, L), 1)
    tri = jnp.where((r64 - c64) * sgn >= 0, 1.0, 0.0).astype(BF)
    row = lax.broadcasted_iota(jnp.int32, (L, GROUP), 0)
    sidx = lax.broadcasted_iota(jnp.int32, (L, GROUP), 1) % L
    diff = (row - sidx) * sgn
    strict = diff > 0
    incl = diff >= 0
    eye = jnp.where(row == sidx, 1.0, 0.0)
    brow = lax.broadcasted_iota(jnp.int32, (GROUP, GROUP), 0) // L
    bcol = lax.broadcasted_iota(jnp.int32, (GROUP, GROUP), 1) // L
    same_head = brow == bcol
    lane128 = lax.broadcasted_iota(jnp.int32, (L, 128), 1)
    half_ones = [jnp.where(lane128 < L, 1.0, 0.0).astype(BF), jnp.where(lane128 >= L, 1.0, 0.0).astype(BF)]
    zero_tile = jnp.zeros((L, 128), BF)

    def bdiag(x):
        xb = x.astype(BF)
        blocks = []
        for h in range(GROUP_HEADS):
            tile = xb[:, 128 * (h // 2):128 * (h // 2 + 1)] * half_ones[h % 2]
            blocks.append(jnp.concatenate([tile, zero_tile] if h < 2 else [zero_tile, tile], axis=1))
        return jnp.concatenate(blocks, axis=0)

    def off_mask(b):
        return ((row // (2 * b)) == (sidx // (2 * b))) & ((row // b) != (sidx // b))

    def stage2(it, src):
        ar_s, t_s, mrb_s, x0_s, o0_s, bk_s, v_s, dec_s = src
        ids = [(bi, gi) for bi in range(nb) for gi in range(N_GROUP)]
        gsl = [slice(gi * GROUP, (gi + 1) * GROUP) for gi in range(N_GROUP)]
        for q in range(steps_per_iter):
            jj = it * steps_per_iter + q
            j = jj + d * (nch - 1 - 2 * jj)
            sl = pl.ds(pl.multiple_of(j * L, L), L)
            s = {k: s_ref[k] for k in ids}
            arh = {k: _dot_t1(ar_s[k[0], j, k[1]], s[k].astype(BF)) for k in ids}
            yield
            u = {k: jnp.dot(t_s[k[0], j, k[1]], bdiag(arh[k][:L] + x0_s[k[0], j, k[1]]),
                            preferred_element_type=F32) for k in ids}
            yield
            upd = {}
            for bi, gi in ids:
                uv = jnp.concatenate([u[(bi, gi)].astype(BF), v_s[bi, j][:, gsl[gi]]], axis=0)
                upd[(bi, gi)] = _dot_t0(uv, bk_s[bi, j, gi])
            for bi, gi in ids:
                dec = dec_s[bi, j]
                s_ref[bi, gi] = s[(bi, gi)] * dec[0:1, gsl[gi]] + jnp.where(same_head, upd[(bi, gi)], 0.0)
            yield
            for bi, gi in ids:
                o = (arh[(bi, gi)][L:] + o0_s[bi, j, gi]
                     - jnp.dot(mrb_s[bi, j, gi], bdiag(u[(bi, gi)]), preferred_element_type=F32))
                o_ref[0, bi, sl, gsl[gi]] = o.astype(BF)
            yield

    def body(it, dst, src):
        ar_s, t_s, mrb_s, x0_s, o0_s, bk_s, v_s, dec_s = dst
        other = stage2(it, src)
        n_slots = 14

        def tick(slot):
            for _ in range((n_yield * (slot + 1)) // n_slots - (n_yield * slot) // n_slots):
                next(other, None)

        chains = []
        for q in range(cpi):
            cid = it * cpi + q
            bi = cid // nch
            j = cid % nch
            sl = pl.ds(pl.multiple_of(j * L, L), L)
            lw = logw_ref[0, bi, sl, :]
            lwh = lw.astype(BF)
            lwl = (lw - lwh.astype(F32)).astype(BF)
            g = jnp.dot(tri, lwh, preferred_element_type=F32) + jnp.dot(tri, lwl, preferred_element_type=F32)
            gtot = jnp.sum(lw, axis=0, keepdims=True)
            e_e = jnp.exp(g - lw)
            e_mg = jnp.exp(-g)
            e_gt = jnp.exp(gtot - g)
            kk = kk_ref[bi, sl, :].astype(F32)
            kka = kka_ref[0, bi, sl, :].astype(F32)
            krep = krep_ref[0, bi, sl, :].astype(F32)
            at = kk * e_e
            bh = kka * e_mg
            kh = krep * e_mg
            rh = r_ref[bi, sl, :].astype(F32) * jnp.exp(g)
            bg = kka * e_gt
            kg = krep * e_gt
            vv = v_ref[bi, sl, :]
            dec_s[bi, j] = jnp.broadcast_to(jnp.exp(gtot), (8, RWKV_WIDTH))
            v_s[bi, j] = vv
            for gi in range(N_GROUP):
                gs = slice(gi * GROUP, (gi + 1) * GROUP)
                ar = jnp.concatenate([at[:, gs], rh[:, gs]], axis=0).astype(BF)
                ar_s[bi, j, gi] = ar
                bk_s[bi, j, gi] = jnp.concatenate([-bg[:, gs], kg[:, gs]], axis=0).astype(BF)
                chains.append(dict(ix=(bi, j, gi), ar=ar, bh=bh[:, gs], kh=kh[:, gs], v=vv[:, gs]))
        tick(0)
        for ch in chains:
            ch['nb'] = _dot_t1(ch['ar'], bdiag(ch['bh']))
        tick(1)
        for ch in chains:
            ch['nk'] = _dot_t1(ch['ar'], bdiag(ch['kh']))
        tick(2)
        for ch in chains:
            nk = ch['nk']
            lhs = jnp.concatenate([jnp.where(strict, nk[:L], 0.0), jnp.where(incl, nk[L:], 0.0)], axis=0)
            xo = jnp.dot(lhs.astype(BF), bdiag(ch['v']), preferred_element_type=F32)
            x0_s[ch['ix']] = xo[:L]
            o0_s[ch['ix']] = xo[L:]
        tick(3)
        for ch in chains:
            nbm = ch['nb']
            mrb_s[ch['ix']] = jnp.where(incl, nbm[L:], 0.0).astype(BF)
            ch['n'] = jnp.where(strict, nbm[:L], 0.0)
            ch['t'] = eye - jnp.where(off_mask(1), ch['n'], 0.0)
        slot = 4
        b = 2
        while b < L:
            om = off_mask(b)
            for ch in chains:
                ch['p'] = jnp.dot(ch['t'].astype(BF), bdiag(jnp.where(om, ch['n'], 0.0)),
                                  preferred_element_type=F32)
            tick(slot)
            for ch in chains:
                ch['t'] = ch['t'] - jnp.dot(ch['p'].astype(BF), bdiag(ch['t']), preferred_element_type=F32)
            tick(slot + 1)
            slot += 2
            b *= 2
        assert slot == n_slots
        for ch in chains:
            t_s[ch['ix']] = ch['t'].astype(BF)
        for _ in other:
            pass

    for par in range(2):
        @pl.when(step % 2 == par)
        def _(par=par):
            def loop_body(it, carry):
                body(it, sets[par], sets[1 - par])
                return carry
            lax.fori_loop(0, n_iter, loop_body, 0)


def rwkv_scan(r, kk, v, krep, kka, logw, bsz, seq):
    t = r.shape[0]
    nb = min(SCAN_BATCH, bsz)
    ts = SCAN_TOKENS // nb
    nc = seq // ts
    nch = ts // CHUNK

    def tile_in(d, c):
        c = jnp.minimum(c, nc - 1)
        return c + d * (nc - 1 - 2 * c)

    def tile_out(d, c):
        c = jnp.maximum(c - 1, 0)
        return c + d * (nc - 1 - 2 * c)

    shared = pl.BlockSpec((nb, ts, RWKV_WIDTH), lambda b, d, c: (b, tile_in(d, c), 0))
    perdir = pl.BlockSpec((1, nb, ts, RWKV_WIDTH), lambda b, d, c: (d, b, tile_in(d, c), 0))
    outspec = pl.BlockSpec((1, nb, ts, RWKV_WIDTH), lambda b, d, c: (d, b, tile_out(d, c), 0))
    shape3 = (bsz, seq, RWKV_WIDTH)
    shape4 = (N_DIR, bsz, seq, RWKV_WIDTH)
    per = (nb, nch, N_GROUP)
    factor_set = [pltpu.VMEM(per + (2 * CHUNK, GROUP), BF),
                  pltpu.VMEM(per + (CHUNK, GROUP), BF),
                  pltpu.VMEM(per + (CHUNK, GROUP), BF),
                  pltpu.VMEM(per + (CHUNK, GROUP), F32),
                  pltpu.VMEM(per + (CHUNK, GROUP), F32),
                  pltpu.VMEM(per + (2 * CHUNK, GROUP), BF),
                  pltpu.VMEM((nb, nch, CHUNK, RWKV_WIDTH), BF),
                  pltpu.VMEM((nb, nch, 8, RWKV_WIDTH), F32)]
    o = pl.pallas_call(
        _scan_kernel,
        grid=(bsz // nb, N_DIR, nc + 1),
        in_specs=[shared, shared, shared, perdir, perdir, perdir],
        out_specs=outspec,
        out_shape=jax.ShapeDtypeStruct(shape4, BF),
        scratch_shapes=[pltpu.VMEM((nb, N_GROUP, GROUP, GROUP), F32)] + factor_set + factor_set,
        compiler_params=_cparams(("parallel", "arbitrary", "arbitrary")),
        name="rwkv_scan",
    )(r.reshape(shape3), kk.reshape(shape3), v.reshape(shape3), krep.reshape(shape4), kka.reshape(shape4),
      logw.reshape(shape4))
    return o.reshape(N_DIR, t, RWKV_WIDTH)


CONV_HALO = 16


def _conv_kernel(tiles_per_seq, h_ref, hp_ref, hn_ref, w_ref, b_ref, lnw_ref, lnb_ref, o_ref, buf, part):
    tm = h_ref.shape[0]
    i = pl.program_id(0)
    first = (i % tiles_per_seq) == 0
    last = (i % tiles_per_seq) == tiles_per_seq - 1
    buf[0:CONV_HALO] = jnp.where(first, 0.0, hp_ref[...].astype(F32))
    buf[CONV_HALO:tm + CONV_HALO] = h_ref[...].astype(F32)
    buf[tm + CONV_HALO:tm + 2 * CONV_HALO] = jnp.where(last, 0.0, hn_ref[...].astype(F32))
    base = CONV_HALO - CONV_KERNEL // 2
    acc = jnp.zeros((tm, CONV_WIDTH), F32) + b_ref[...]
    for b in range(8):
        pb = None
        for a in range((base + CONV_KERNEL - 1) // 8 + 1):
            k = 8 * a + b - base
            if 0 <= k < CONV_KERNEL:
                term = w_ref[k:k + 1, :] * buf[8 * a:8 * a + tm + 8, :]
                pb = term if pb is None else pb + term
        part[...] = pb
        acc = acc + part[b:b + tm, :]
    mu = jnp.mean(acc, axis=-1, keepdims=True)
    xc = acc - mu
    var = jnp.mean(xc * xc, axis=-1, keepdims=True)
    y = xc * lax.rsqrt(var + LN_EPS) * lnw_ref[...] + lnb_ref[...]
    o_ref[...] = (y * _sigmoid(y)).astype(BF)


def conformer_conv(hc, seq, p, l):
    t = hc.shape[0]
    tm = TOKEN_TILE
    nbh = tm // CONV_HALO
    row = pl.BlockSpec((tm, CONV_WIDTH), lambda i: (i, 0))
    vec = lambda a: a.reshape(1, -1).astype(F32)
    return pl.pallas_call(
        functools.partial(_conv_kernel, seq // tm),
        grid=(t // tm,),
        in_specs=[row,
                  pl.BlockSpec((CONV_HALO, CONV_WIDTH), lambda i: (jnp.maximum(i * nbh - 1, 0), 0)),
                  pl.BlockSpec((CONV_HALO, CONV_WIDTH),
                               lambda i: (jnp.minimum((i + 1) * nbh, t // CONV_HALO - 1), 0)),
                  _const_spec((CONV_KERNEL, CONV_WIDTH)), _const_spec((1, CONV_WIDTH)),
                  _const_spec((1, CONV_WIDTH)), _const_spec((1, CONV_WIDTH))],
        out_specs=row,
        out_shape=jax.ShapeDtypeStruct((t, CONV_WIDTH), BF),
        scratch_shapes=[pltpu.VMEM((tm + 2 * CONV_HALO, CONV_WIDTH), F32),
                        pltpu.VMEM((tm + 8, CONV_WIDTH), F32)],
        compiler_params=_cparams(("parallel",)),
        name="conformer_conv",
    )(hc, hc, hc, p['conv_dw_w'][l], vec(p['conv_dw_b'][l]), vec(p['conv_ln_w'][l]), vec(p['conv_ln_b'][l]))


def _merge_kernel(x_ref, ya_ref, o_ref, bonus_ref, g_ref, yc_ref, gate_ref, ones_ref, gnw_ref, gnb_ref,
                  wa_ref, wb_ref, wc_ref, wo_ref, out_ref):
    ones = ones_ref[...]
    o = o_ref[0].astype(F32) + o_ref[1].astype(F32)
    inv = 1.0 / RWKV_HEAD_DIM
    mu = _bdot(o, ones) * inv
    oc = o - mu
    var = _bdot(oc * oc, ones) * inv
    on = oc * lax.rsqrt(var + GN_EPS) * gnw_ref[...] + gnb_ref[...]
    yb = (on + bonus_ref[...]) * g_ref[...]
    m = gate_ref[:, 0:D_MODEL].astype(F32) * _bdot(ya_ref[...], wa_ref[...])
    m = m + gate_ref[:, D_MODEL:2 * D_MODEL].astype(F32) * _bdot(yb, wb_ref[...])
    m = m + gate_ref[:, 2 * D_MODEL:3 * D_MODEL].astype(F32) * _bdot(yc_ref[...], wc_ref[...])
    out_ref[...] = x_ref[...] + _bdot(m, wo_ref[...])


def merge(x, ya, o, bonus, g, yc, gates, p, l):
    t = x.shape[0]
    tm = TOKEN_TILE
    row = lambda w: pl.BlockSpec((tm, w), lambda i: (i, 0))
    vec = lambda a: a.reshape(1, -1).astype(F32)
    return pl.pallas_call(
        _merge_kernel,
        grid=(t // tm,),
        in_specs=[row(D_MODEL), row(FNET_WIDTH),
                  pl.BlockSpec((N_DIR, tm, RWKV_WIDTH), lambda i: (0, i, 0)),
                  row(RWKV_WIDTH), row(RWKV_WIDTH), row(CONV_WIDTH), row(3 * D_MODEL),
                  _const_spec((RWKV_WIDTH, RWKV_WIDTH)), _const_spec((1, RWKV_WIDTH)), _const_spec((1, RWKV_WIDTH)),
                  _const_spec((FNET_WIDTH, D_MODEL)), _const_spec((RWKV_WIDTH, D_MODEL)),
                  _const_spec((CONV_WIDTH, D_MODEL)), _const_spec((D_MODEL, D_MODEL))],
        out_specs=row(D_MODEL),
        out_shape=jax.ShapeDtypeStruct((t, D_MODEL), F32),
        compiler_params=_cparams(("parallel",)),
        name="merge",
    )(x, ya, o, bonus, g, yc, gates, _head_ones(), vec(p['rwkv_gn_w'][l]), vec(p['rwkv_gn_b'][l]),
      p['fnet_w'][l].astype(BF), p['rwkv_w_o'][l].astype(BF), p['conv_w_o'][l].astype(BF),
      p['mix_w_out'][l].astype(BF))


FFN_HALO = 16


def _erf(x):
    return lax.erf(x)


def _ffn_kernel(tiles_per_seq, final, x_ref, xp_ref, xn_ref, g_ref, wup_ref, dww_ref, dwb_ref, wdn_ref, gf_ref,
                out_ref, xn_s, h_s):
    tm = x_ref.shape[0]
    i = pl.program_id(0)
    first = (i % tiles_per_seq) == 0
    last = (i % tiles_per_seq) == tiles_per_seq - 1
    g = g_ref[...]
    hl = FFN_HALO
    xn_s[0:hl] = _rms(xp_ref[...], g).astype(BF)
    xn_s[hl:tm + hl] = _rms(x_ref[...], g).astype(BF)
    xn_s[tm + hl:tm + 2 * hl] = _rms(xn_ref[...], g).astype(BF)
    rows = lax.broadcasted_iota(jnp.int32, (tm + 2 * hl, 1), 0)
    pad = (first & (rows < hl)) | (last & (rows >= tm + hl))
    acc = jnp.zeros((tm, D_MODEL), F32)
    for lo, hi in FF_BLOCKS:
        h = jnp.dot(xn_s[...], wup_ref[:, lo:hi], preferred_element_type=F32)
        h_s[:, 0:hi - lo] = jnp.where(pad, 0.0, h)
        hc = (dww_ref[0:1, lo:hi] * h_s[hl - 1:tm + hl - 1, 0:hi - lo]
              + dww_ref[1:2, lo:hi] * h_s[hl:tm + hl, 0:hi - lo]
              + dww_ref[2:3, lo:hi] * h_s[hl + 1:tm + hl + 1, 0:hi - lo] + dwb_ref[:, lo:hi])
        gate = jnp.dot(xn_s[hl:tm + hl], wup_ref[:, D_FF + lo:D_FF + hi], preferred_element_type=F32)
        act = 0.5 * hc * (1.0 + _erf(hc * (1.0 / math.sqrt(2.0)))) * gate
        acc = acc + jnp.dot(act.astype(BF), wdn_ref[lo:hi, :], preferred_element_type=F32)
    y = x_ref[...] + acc
    if final:
        y = _rms(y, gf_ref[...])
    out_ref[...] = y


def ffn(x, seq, p, l, final):
    t = x.shape[0]
    tm = TOKEN_TILE
    hl = FFN_HALO
    nbh = tm // hl
    row = pl.BlockSpec((tm, D_MODEL), lambda i: (i, 0))
    vec = lambda a: a.reshape(1, -1).astype(F32)
    fb = max(hi - lo for lo, hi in FF_BLOCKS)
    return pl.pallas_call(
        functools.partial(_ffn_kernel, seq // tm, final),
        grid=(t // tm,),
        in_specs=[row,
                  pl.BlockSpec((hl, D_MODEL), lambda i: (jnp.maximum(i * nbh - 1, 0), 0)),
                  pl.BlockSpec((hl, D_MODEL), lambda i: (jnp.minimum((i + 1) * nbh, t // hl - 1), 0)),
                  _const_spec((1, D_MODEL)), _const_spec((D_MODEL, 2 * D_FF)),
                  _const_spec((3, D_FF)), _const_spec((1, D_FF)), _const_spec((D_FF, D_MODEL)),
                  _const_spec((1, D_MODEL))],
        out_specs=row,
        out_shape=jax.ShapeDtypeStruct((t, D_MODEL), F32),
        scratch_shapes=[pltpu.VMEM((tm + 2 * hl, D_MODEL), BF), pltpu.VMEM((tm + 2 * hl, fb), F32)],
        compiler_params=_cparams(("parallel",)),
        name="ffn",
    )(x, x, x, vec(p['ffn_norm_g'][l]), p['ffn_w_up'][l].astype(BF), p['ffn_dw_w'][l],
      vec(p['ffn_dw_b'][l]), p['ffn_w_down'][l].astype(BF), vec(p['final_norm_g']))


def encoder(x3, p):
    bsz, seq, _ = x3.shape
    depth = p['w_in'].shape[0]
    x = x3.reshape(bsz * seq, D_MODEL)
    tables = _fft_tables(seq)
    for l in range(depth):
        ua, ub, hc, gates = mixer_in(x, p['attn_norm_g'][l], p['w_in'][l].astype(BF))
        ya = fourier_mix(ua, bsz, seq, tables)
        r, kk, v, krep, kka, logw, bonus, g = rwkv_prep(ub, seq, p, l)
        o = rwkv_scan(r, kk, v, krep, kka, logw, bsz, seq)
        yc = conformer_conv(hc, seq, p, l)
        x = merge(x, ya, o, bonus, g, yc, gates, p, l)
        x = ffn(x, seq, p, l, final=(l == depth - 1))
    return x.reshape(bsz, seq, D_MODEL)


def kernel(x_prompt, x_sample, attn_norm_g, w_in, fnet_w, rwkv_mu_prev, rwkv_mu_next, rwkv_w0, rwkv_w_up,
           rwkv_a0, rwkv_a_up, rwkv_g_up, rwkv_k_k, rwkv_k_a, rwkv_r_k, rwkv_gn_w, rwkv_gn_b, rwkv_w_o,
           conv_dw_w, conv_dw_b, conv_ln_w, conv_ln_b, conv_w_o, mix_w_out, ffn_norm_g, ffn_w_up,
           ffn_dw_w, ffn_dw_b, ffn_w_down, final_norm_g):
    p = dict(attn_norm_g=attn_norm_g, w_in=w_in, fnet_w=fnet_w, rwkv_mu_prev=rwkv_mu_prev,
             rwkv_mu_next=rwkv_mu_next, rwkv_w0=rwkv_w0, rwkv_w_up=rwkv_w_up, rwkv_a0=rwkv_a0,
             rwkv_a_up=rwkv_a_up, rwkv_g_up=rwkv_g_up, rwkv_k_k=rwkv_k_k, rwkv_k_a=rwkv_k_a, rwkv_r_k=rwkv_r_k,
             rwkv_gn_w=rwkv_gn_w, rwkv_gn_b=rwkv_gn_b, rwkv_w_o=rwkv_w_o, conv_dw_w=conv_dw_w,
             conv_dw_b=conv_dw_b, conv_ln_w=conv_ln_w, conv_ln_b=conv_ln_b, conv_w_o=conv_w_o,
             mix_w_out=mix_w_out, ffn_norm_g=ffn_norm_g, ffn_w_up=ffn_w_up, ffn_dw_w=ffn_dw_w,
             ffn_dw_b=ffn_dw_b, ffn_w_down=ffn_w_down, final_norm_g=final_norm_g)
    return (encoder(x_prompt, p), encoder(x_sample, p))
```
